```python
import math
import jax, jax.numpy as jnp
from jax import lax
import numpy as np

D_MODEL = 4096
BATCH = 1
SEQ = 8192
DEPTH = 2

CTX_LEN = 256
GRID_W = 64

N_HEADS = 16
NOPE_DIM = 128
ROPE_DIM = 64
V_DIM = 128
Q_RANK = 1024
KV_RANK = 512
MLA_W = N_HEADS * V_DIM
ROPE_BASE = 10000.0
Q_BLOCK = 128
ATTN_SCALE = 1.0 / math.sqrt(NOPE_DIM + ROPE_DIM)

SSM_W = 2048
GROUP_CH = 16
N_GROUPS = SSM_W // GROUP_CH
STATE_N = 64
DT_MIN = 0.001
DT_MAX = 0.1

CONV_W = 2048
CONV_K = 31

NORM_EPS = 1e-6
N_BRANCH = 3

COL_LAYOUT = (("kv", KV_RANK), ("kr", ROPE_DIM), ("su", SSM_W), ("q", Q_RANK), ("mg", MLA_W), ("sg", SSM_W), ("ca", CONV_W), ("cb", CONV_W), ("cg", CONV_W), ("gm", D_MODEL), ("gs", D_MODEL), ("gc", D_MODEL))
CTX_COLS = KV_RANK + ROPE_DIM + SSM_W
IN_COLS = CTX_COLS + Q_RANK + MLA_W + SSM_W + 3 * CONV_W + N_BRANCH * D_MODEL

kernel_name = "hybrid_mla_s5_conformer_dit_block"


def rms_norm(x, g):
    xf = x.astype(jnp.float32)
    y = xf * lax.rsqrt(jnp.mean(xf * xf, axis=-1, keepdims=True) + NORM_EPS)
    return (y * g.astype(jnp.float32)).astype(x.dtype)


def layer_norm(x, g, b):
    xf = x.astype(jnp.float32)
    mu = jnp.mean(xf, axis=-1, keepdims=True)
    var = jnp.mean(jnp.square(xf - mu), axis=-1, keepdims=True)
    y = (xf - mu) * lax.rsqrt(var + NORM_EPS)
    return (y * g.astype(jnp.float32) + b.astype(jnp.float32)).astype(x.dtype)


def split_cols(p):
    parts, off = {}, 0
    for name, size in COL_LAYOUT:
        if off + size > p.shape[-1]:
            break
        parts[name] = p[..., off:off + size]
        off += size
    return parts


def axial_rope_tables(row, col):
    half = ROPE_DIM // 2
    inv = 1.0 / (ROPE_BASE ** (jnp.arange(0, half, 2, dtype=jnp.float32) / half))
    ang = jnp.stack([row[:, None] * inv, col[:, None] * inv], axis=1)
    return jnp.cos(ang), jnp.sin(ang)


def apply_axial_rope(x, cos, sin):
    xs = x.reshape(x.shape[:-1] + (2, 2, ROPE_DIM // 4))
    x1, x2 = xs[..., 0, :], xs[..., 1, :]
    cs, sn = cos[:, None], sin[:, None]
    out = jnp.stack([x1 * cs - x2 * sn, x1 * sn + x2 * cs], axis=-2)
    return out.reshape(x.shape).astype(x.dtype)


def mla_keys_values(kv_down, kv_norm, w_ukv):
    batch, n, _ = kv_down.shape
    kv = (rms_norm(kv_down, kv_norm) @ w_ukv).reshape(batch, n, N_HEADS, NOPE_DIM + V_DIM)
    return kv[..., :NOPE_DIM], kv[..., NOPE_DIM:]


def mla_queries(q_down, q_norm, w_uq):
    batch, n, _ = q_down.shape
    q = (rms_norm(q_down, q_norm) @ w_uq).reshape(batch, n, N_HEADS, NOPE_DIM + ROPE_DIM)
    return q[..., :NOPE_DIM], q[..., NOPE_DIM:]


def attend(qn, qr, kn, kr, v):
    s = (jnp.einsum("bqhd,bkhd->bhqk", qn, kn) + jnp.einsum("bqhr,bkr->bhqk", qr, kr)) * ATTN_SCALE
    p = jax.nn.softmax(s.astype(jnp.float32), axis=-1).astype(v.dtype)
    return jnp.einsum("bhqk,bkhd->bqhd", p, v)


def latent_attention(qn, qr, kn, kr, v):
    batch, seq, heads, _ = qn.shape
    nblk = seq // Q_BLOCK

    def to_blocks(t):
        return t.reshape((batch, nblk, Q_BLOCK) + t.shape[2:]).swapaxes(0, 1)

    o = lax.map(lambda qb: attend(qb[0], qb[1], kn, kr, v), (to_blocks(qn), to_blocks(qr)))
    return o.swapaxes(0, 1).reshape(batch, seq, heads * V_DIM)


def zoh_discretize(lam_re, lam_im, log_dt, b_re, b_im):
    lam = lax.complex(lam_re.astype(jnp.float32), lam_im.astype(jnp.float32))
    dt = jnp.exp(log_dt.astype(jnp.float32))[:, None]
    lam_bar = jnp.exp(lam * dt)
    b = lax.complex(b_re.astype(jnp.float32), b_im.astype(jnp.float32))
    return lam_bar, ((lam_bar - 1.0) / lam)[..., None] * b


def diag_scan(lam_bar, bu, h0):
    bu = bu.at[:, 0].add(lam_bar * h0)
    a = jnp.broadcast_to(lam_bar, bu.shape)

    def combine(left, right):
        a_l, b_l = left
        a_r, b_r = right
        return a_l * a_r, a_r * b_l + b_r

    _, h = lax.associative_scan(combine, (a, bu), axis=1)
    return h


def ssm_readout(c_mat, h):
    return jnp.real(jnp.einsum("gcn,blgn->blgc", c_mat, h)).reshape(h.shape[0], h.shape[1], SSM_W)


def s5_glu(y, w_glu, b_glu):
    z = jax.nn.gelu(y)
    return z * jax.nn.sigmoid(z @ w_glu + b_glu)


def s5_branch(u_x, u_c, lam_re, lam_im, log_dt, b_re, b_im, c_re, c_im, d_skip, w_glu, b_glu, need_ctx_out):
    batch, seq, _ = u_x.shape
    n_ctx = u_c.shape[1]
    ux = u_x.astype(jnp.float32)
    uc = u_c.astype(jnp.float32)
    ux_g = ux.reshape(batch, seq, N_GROUPS, GROUP_CH).astype(jnp.complex64)
    uc_g = uc.reshape(batch, n_ctx, N_GROUPS, GROUP_CH).astype(jnp.complex64)
    d = d_skip.astype(jnp.float32)
    y_x = d * ux
    y_c = d * uc if need_ctx_out else None
    h_zero = jnp.zeros((batch, N_GROUPS, STATE_N), jnp.complex64)
    for direction in range(2):
        lam_bar, b_bar = zoh_discretize(lam_re[direction], lam_im[direction], log_dt[direction], b_re[direction], b_im[direction])
        c_mat = lax.complex(c_re[direction].astype(jnp.float32), c_im[direction].astype(jnp.float32))
        bu_x = jnp.einsum("gnc,blgc->blgn", b_bar, ux_g)
        bu_c = jnp.einsum("gnc,blgc->blgn", b_bar, uc_g)
        if direction == 1:
            bu_x, bu_c = jnp.flip(bu_x, axis=1), jnp.flip(bu_c, axis=1)
        h_c = diag_scan(lam_bar, bu_c, h_zero)
        h_x = diag_scan(lam_bar, bu_x, h_c[:, -1])
        if direction == 1:
            h_x = jnp.flip(h_x, axis=1)
            if need_ctx_out:
                h_c = jnp.flip(h_c, axis=1)
        y_x = y_x + ssm_readout(c_mat, h_x)
        if need_ctx_out:
            y_c = y_c + ssm_readout(c_mat, h_c)
    out_x = s5_glu(y_x.astype(u_x.dtype), w_glu, b_glu)
    out_c = s5_glu(y_c.astype(u_c.dtype), w_glu, b_glu) if need_ctx_out else None
    return out_x, out_c


def conformer_conv(a, b, w_dw, b_dw, ln_g, ln_b, w_pw, b_pw):
    u = a * jax.nn.sigmoid(b)
    u = lax.conv_general_dilated(u, w_dw[:, None, :].astype(u.dtype), window_strides=(1,), padding=((CONV_K // 2, CONV_K // 2),), dimension_numbers=("NWC", "WIO", "NWC"), feature_group_count=CONV_W) + b_dw
    u = jax.nn.silu(layer_norm(u, ln_g, ln_b))
    return u @ w_pw + b_pw


def merge_branches(p, o_mla, o_ssm, o_conv, w_bm, w_bs, w_bc, w_out):
    merged = (jax.nn.sigmoid(p["gm"]) * ((o_mla * jax.nn.silu(p["mg"])) @ w_bm)
              + jax.nn.sigmoid(p["gs"]) * ((o_ssm * jax.nn.silu(p["sg"])) @ w_bs)
              + jax.nn.sigmoid(p["gc"]) * ((o_conv * jax.nn.silu(p["cg"])) @ w_bc))
    return merged @ w_out


def setup_inputs(seed: int = 0) -> dict:
    key = jax.random.key(seed)
    ks = iter(jax.random.split(key, 40))
    f32 = jnp.float32
    L = DEPTH

    def nrm(shape, scale):
        return jax.random.normal(next(ks), shape, f32) * scale

    n_idx = jnp.arange(STATE_N, dtype=f32)
    return {
        "x": nrm((BATCH, SEQ, D_MODEL), 1.0),
        "c": nrm((BATCH, D_MODEL), 1.0),
        "ctx": nrm((BATCH, CTX_LEN, D_MODEL), 1.0),
        "c_ctx": nrm((D_MODEL,), 1.0),
        "w_mod": nrm((L, D_MODEL, 3 * D_MODEL), 0.5 * D_MODEL ** -0.5),
        "b_mod": nrm((L, 3 * D_MODEL), 0.01),
        "g_pre": 1.0 + nrm((L, D_MODEL), 0.01),
        "g_post": 1.0 + nrm((L, D_MODEL), 0.01),
        "w_in": nrm((L, D_MODEL, IN_COLS), D_MODEL ** -0.5),
        "q_norm": 1.0 + nrm((L, Q_RANK), 0.01),
        "kv_norm": 1.0 + nrm((L, KV_RANK), 0.01),
        "w_uq": nrm((L, Q_RANK, N_HEADS * (NOPE_DIM + ROPE_DIM)), Q_RANK ** -0.5),
        "w_ukv": nrm((L, KV_RANK, N_HEADS * (NOPE_DIM + V_DIM)), KV_RANK ** -0.5),
        "lam_re": -0.5 + nrm((L, 2, N_GROUPS, STATE_N), 0.01),
        "lam_im": math.pi * n_idx + nrm((L, 2, N_GROUPS, STATE_N), 0.01),
        "log_dt": jax.random.uniform(next(ks), (L, 2, N_GROUPS), f32, math.log(DT_MIN), math.log(DT_MAX)),
        "b_re": nrm((L, 2, N_GROUPS, STATE_N, GROUP_CH), (2 * GROUP_CH) ** -0.5),
        "b_im": nrm((L, 2, N_GROUPS, STATE_N, GROUP_CH), (2 * GROUP_CH) ** -0.5),
        "c_re": nrm((L, 2, N_GROUPS, GROUP_CH, STATE_N), (2 * STATE_N) ** -0.5),
        "c_im": nrm((L, 2, N_GROUPS, GROUP_CH, STATE_N), (2 * STATE_N) ** -0.5),
        "d_skip": nrm((L, SSM_W), 1.0),
        "w_glu": nrm((L, SSM_W, SSM_W), SSM_W ** -0.5),
        "b_glu": nrm((L, SSM_W), 0.01),
        "w_dw": nrm((L, CONV_K, CONV_W), CONV_K ** -0.5),
        "b_dw": nrm((L, CONV_W), 0.01),
        "ln_g": 1.0 + nrm((L, CONV_W), 0.01),
        "ln_b": nrm((L, CONV_W), 0.01),
        "w_pw": nrm((L, CONV_W, CONV_W), CONV_W ** -0.5),
        "b_pw": nrm((L, CONV_W), 0.01),
        "w_bm": nrm((L, MLA_W, D_MODEL), MLA_W ** -0.5),
        "w_bs": nrm((L, SSM_W, D_MODEL), SSM_W ** -0.5),
        "w_bc": nrm((L, CONV_W, D_MODEL), CONV_W ** -0.5),
        "w_out": nrm((L, D_MODEL, D_MODEL), D_MODEL ** -0.5),
    }


def reference(x, c, ctx, c_ctx, w_mod, b_mod, g_pre, g_post, w_in, q_norm, kv_norm, w_uq, w_ukv,
              lam_re, lam_im, log_dt, b_re, b_im, c_re, c_im, d_skip, w_glu, b_glu,
              w_dw, b_dw, ln_g, ln_b, w_pw, b_pw, w_bm, w_bs, w_bc, w_out):
    batch, seq, _ = x.shape
    n_ctx = ctx.shape[1]
    rows = seq // GRID_W
    row = jnp.repeat(jnp.arange(rows, dtype=jnp.float32), GRID_W)
    col = jnp.tile(jnp.arange(GRID_W, dtype=jnp.float32), rows)
    cos, sin = axial_rope_tables(row, col)
    act_x = jax.nn.silu(c)
    act_c = jax.nn.silu(c_ctx)
    for l in range(DEPTH):
        last = l == DEPTH - 1
        mod_x = act_x @ w_mod[l] + b_mod[l]
        mod_c = act_c @ w_mod[l] + b_mod[l]
        shift_x, scale_x, gate_x = jnp.split(mod_x[:, None, :], 3, axis=-1)
        shift_c, scale_c, gate_c = jnp.split(mod_c, 3, axis=-1)
        hx = rms_norm(x, g_pre[l]) * (1.0 + scale_x) + shift_x
        hc = rms_norm(ctx, g_pre[l]) * (1.0 + scale_c) + shift_c
        px = split_cols(hx @ w_in[l])
        pc = split_cols(hc @ (w_in[l][:, :CTX_COLS] if last else w_in[l]))

        kn_c, v_c = mla_keys_values(pc["kv"], kv_norm[l], w_ukv[l])
        kn_x, v_x = mla_keys_values(px["kv"], kv_norm[l], w_ukv[l])
        kr_c = pc["kr"]
        kr_x = apply_axial_rope(px["kr"][:, :, None, :], cos, sin)[:, :, 0, :]
        qn_x, qr_x = mla_queries(px["q"], q_norm[l], w_uq[l])
        qr_x = apply_axial_rope(qr_x, cos, sin)
        o_mla_x = latent_attention(qn_x, qr_x, jnp.concatenate([kn_c, kn_x], axis=1), jnp.concatenate([kr_c, kr_x], axis=1), jnp.concatenate([v_c, v_x], axis=1))

        o_ssm_x, o_ssm_c = s5_branch(px["su"], pc["su"], lam_re[l], lam_im[l], log_dt[l], b_re[l], b_im[l], c_re[l], c_im[l], d_skip[l], w_glu[l], b_glu[l], not last)

        o_conv_x = conformer_conv(px["ca"], px["cb"], w_dw[l], b_dw[l], ln_g[l], ln_b[l], w_pw[l], b_pw[l])

        x_new = x + gate_x * rms_norm(merge_branches(px, o_mla_x, o_ssm_x, o_conv_x, w_bm[l], w_bs[l], w_bc[l], w_out[l]), g_post[l])
        if not last:
            qn_c, qr_c = mla_queries(pc["q"], q_norm[l], w_uq[l])
            o_mla_c = attend(qn_c, qr_c, kn_c, kr_c, v_c).reshape(batch, n_ctx, MLA_W)
            o_conv_c = conformer_conv(pc["ca"], pc["cb"], w_dw[l], b_dw[l], ln_g[l], ln_b[l], w_pw[l], b_pw[l])
            ctx = ctx + gate_c * rms_norm(merge_branches(pc, o_mla_c, o_ssm_c, o_conv_c, w_bm[l], w_bs[l], w_bc[l], w_out[l]), g_post[l])
        x = x_new
    return x
```

```python
import functools
import math

import jax
import jax.numpy as jnp
from jax import lax
from jax.experimental import pallas as pl
from jax.experimental.pallas import tpu as pltpu

N_HEADS = 16
NOPE_DIM = 128
ROPE_DIM = 64
V_DIM = 128
QK_DIM = NOPE_DIM + ROPE_DIM
ROPE_BASE = 10000.0
GRID_W = 64
ATTN_SCALE = 1.0 / math.sqrt(NOPE_DIM + ROPE_DIM)
GROUP_CH = 16
STATE_N = 64
CONV_K = 31
CONV_HALO = 16
NORM_EPS = 1e-6

LANES = 128
SUBLANES = 8
SLAB_GROUPS = 16
SLAB_W = SLAB_GROUPS * GROUP_CH
SLAB_STATES = SLAB_GROUPS * STATE_N
VMEM_LIMIT = 56 * 1024 * 1024

F32 = jnp.float32
BF16 = jnp.bfloat16


def _pick(n, cands):
    for c in cands:
        if n % c == 0:
            return c
    raise ValueError(f"no tile for {n} in {cands}")


def _cparams(*sem):
    return pltpu.CompilerParams(dimension_semantics=sem, vmem_limit_bytes=VMEM_LIMIT)


def _silu(v):
    return v * jax.nn.sigmoid(v)


def _rms(v, g):
    return v * lax.rsqrt(jnp.mean(v * v, axis=-1, keepdims=True) + NORM_EPS) * g


def _mod_kernel(c_ref, w_ref, b_ref, o_ref):
    a = _silu(c_ref[...])
    o_ref[...] = jnp.dot(a.astype(BF16), w_ref[...].astype(BF16), preferred_element_type=F32) + b_ref[...]


def _modulation(cvec, w_mod, b_mod3, layer):
    d = cvec.shape[1]
    n = w_mod.shape[2]
    tn = _pick(n, (512, 256, 128))
    return pl.pallas_call(
        _mod_kernel,
        grid=(n // tn,),
        in_specs=[
            pl.BlockSpec((SUBLANES, d), lambda j: (0, 0)),
            pl.BlockSpec((None, d, tn), lambda j: (layer, 0, j)),
            pl.BlockSpec((None, 1, tn), lambda j: (layer, 0, j)),
        ],
        out_specs=pl.BlockSpec((SUBLANES, tn), lambda j: (0, j)),
        out_shape=jax.ShapeDtypeStruct((SUBLANES, n), F32),
        compiler_params=_cparams("parallel"),
        name="modulation",
    )(cvec, w_mod, b_mod3)


def _row_select(i, tm, n_ctx, ctx_row, x_row):
    rows = i * tm + lax.broadcasted_iota(jnp.int32, (tm, 1), 0)
    return jnp.where(rows < n_ctx, ctx_row, x_row)


def _prenorm_kernel(z_ref, g_ref, sh_ref, sc_ref, o_ref, *, tm, n_ctx):
    i = pl.program_id(0)
    y = _rms(z_ref[...], g_ref[...])
    shift = _row_select(i, tm, n_ctx, sh_ref[0:1, :], sh_ref[1:2, :])
    scale = _row_select(i, tm, n_ctx, sc_ref[0:1, :], sc_ref[1:2, :])
    o_ref[...] = (y * (1.0 + scale) + shift).astype(BF16)


def _prenorm(z, g, shift, scale, n_ctx):
    m, d = z.shape
    tm = _pick(m, (256, 128))
    vec = pl.BlockSpec((SUBLANES, d), lambda i: (0, 0))
    return pl.pallas_call(
        functools.partial(_prenorm_kernel, tm=tm, n_ctx=n_ctx),
        grid=(m // tm,),
        in_specs=[pl.BlockSpec((tm, d), lambda i: (i, 0)), pl.BlockSpec((1, d), lambda i: (0, 0)), vec, vec],
        out_specs=pl.BlockSpec((tm, d), lambda i: (i, 0)),
        out_shape=jax.ShapeDtypeStruct((m, d), BF16),
        compiler_params=_cparams("parallel"),
        name="prenorm",
    )(z, g, shift, scale)


def _inproj_kernel(h_ref, w_ref, o_ref):
    o_ref[...] = jnp.dot(h_ref[...], w_ref[...], preferred_element_type=F32).astype(o_ref.dtype)


def _inproj(h, w):
    m, k = h.shape
    n = w.shape[1]
    tm = _pick(m, (768, 640, 512, 256))
    tn = _pick(n, (512, 256))
    return pl.pallas_call(
        _inproj_kernel,
        grid=(m // tm, n // tn),
        in_specs=[pl.BlockSpec((tm, k), lambda i, j: (i, 0)), pl.BlockSpec((k, tn), lambda i, j: (0, j))],
        out_specs=pl.BlockSpec((tm, tn), lambda i, j: (i, j)),
        out_shape=jax.ShapeDtypeStruct((m, n), BF16),
        compiler_params=_cparams("parallel", "arbitrary"),
        name="inproj",
    )(h, w)


def _rope_mix(y2, cos, sin):
    return y2 * cos + pltpu.roll(y2, ROPE_DIM, 1) * sin


def _qup_kernel(p_ref, g_ref, w_ref, cos_ref, sin_ref, o_ref, qn_ref):
    @pl.when(pl.program_id(1) == 0)
    def _():
        qn_ref[...] = _rms(p_ref[...].astype(F32), g_ref[...]).astype(BF16)

    y = jnp.dot(qn_ref[...], w_ref[...], preferred_element_type=F32)
    r = _rope_mix(y[:, NOPE_DIM:], cos_ref[...], sin_ref[...])
    o_ref[:, 0:NOPE_DIM] = (y[:, 0:NOPE_DIM] * ATTN_SCALE).astype(BF16)
    o_ref[:, NOPE_DIM:QK_DIM] = (r[:, 0:ROPE_DIM] * ATTN_SCALE).astype(BF16)


def _qup(p, g, wq, cos, sin, q_rank):
    m = p.shape[0]
    heads = wq.shape[0]
    tm = _pick(m, (768, 640, 512, 256))
    return pl.pallas_call(
        _qup_kernel,
        grid=(m // tm, heads),
        in_specs=[
            pl.BlockSpec((tm, q_rank), lambda i, h: (i, 0)),
            pl.BlockSpec((1, q_rank), lambda i, h: (0, 0)),
            pl.BlockSpec((None, q_rank, 2 * LANES), lambda i, h: (h, 0, 0)),
            pl.BlockSpec((tm, LANES), lambda i, h: (i, 0)),
            pl.BlockSpec((tm, LANES), lambda i, h: (i, 0)),
        ],
        out_specs=pl.BlockSpec((None, tm, QK_DIM), lambda i, h: (h, i, 0)),
        out_shape=jax.ShapeDtypeStruct((heads, m, QK_DIM), BF16),
        scratch_shapes=[pltpu.VMEM((tm, q_rank), BF16)],
        compiler_params=_cparams("parallel", "arbitrary"),
        name="q_up",
    )(p, g, wq, cos, sin)


def _kvup_kernel(p_ref, kr_ref, g_ref, w_ref, cos_ref, sin_ref, k_ref, v_ref, kvn_ref, krr_ref):
    @pl.when(pl.program_id(1) == 0)
    def _():
        kvn_ref[...] = _rms(p_ref[...].astype(F32), g_ref[...]).astype(BF16)
        krr_ref[...] = _rope_mix(kr_ref[...].astype(F32), cos_ref[...], sin_ref[...]).astype(BF16)

    y = jnp.dot(kvn_ref[...], w_ref[...], preferred_element_type=F32)
    k_ref[:, 0:NOPE_DIM] = y[:, 0:NOPE_DIM].astype(BF16)
    k_ref[:, NOPE_DIM:QK_DIM] = krr_ref[:, 0:ROPE_DIM]
    v_ref[...] = y[:, NOPE_DIM:].astype(BF16)


def _kvup(p, g, wkv, cos, sin, q_rank, kv_rank):
    m = p.shape[0]
    heads = wkv.shape[0]
    tm = _pick(m, (768, 640, 512, 256))
    kv_blk = q_rank // kv_rank
    kr_blk = (q_rank + kv_rank) // LANES
    return pl.pallas_call(
        _kvup_kernel,
        grid=(m // tm, heads),
        in_specs=[
            pl.BlockSpec((tm, kv_rank), lambda i, h: (i, kv_blk)),
            pl.BlockSpec((tm, LANES), lambda i, h: (i, kr_blk)),
            pl.BlockSpec((1, kv_rank), lambda i, h: (0, 0)),
            pl.BlockSpec((None, kv_rank, 2 * LANES), lambda i, h: (h, 0, 0)),
            pl.BlockSpec((tm, LANES), lambda i, h: (i, 0)),
            pl.BlockSpec((tm, LANES), lambda i, h: (i, 0)),
        ],
        out_specs=[
            pl.BlockSpec((None, tm, QK_DIM), lambda i, h: (h, i, 0)),
            pl.BlockSpec((None, tm, V_DIM), lambda i, h: (h, i, 0)),
        ],
        out_shape=[
            jax.ShapeDtypeStruct((heads, m, QK_DIM), BF16),
            jax.ShapeDtypeStruct((heads, m, V_DIM), BF16),
        ],
        scratch_shapes=[pltpu.VMEM((tm, kv_rank), BF16), pltpu.VMEM((tm, LANES), BF16)],
        compiler_params=_cparams("parallel", "arbitrary"),
        name="kv_up",
    )(p, p, g, wkv, cos, sin)


def _attn_kernel(q_ref, k_ref, v_ref, mg_ref, o_ref, *, tq, tk, n_ctx, n_kx):
    qi = pl.program_id(1)
    q = q_ref[...]

    def step(off, size, carry):
        m_old, l_old, acc = carry
        k = k_ref[pl.ds(off, size), :]
        v = v_ref[pl.ds(off, size), :]
        s = lax.dot_general(q, k, (((1,), (1,)), ((), ())), preferred_element_type=F32)
        m_new = jnp.maximum(m_old, jnp.max(s, axis=-1, keepdims=True))
        alpha = jnp.exp(m_old - m_new)
        p = jnp.exp(s - m_new)
        l_new = alpha * l_old + jnp.sum(p, axis=-1, keepdims=True)
        acc = alpha * acc + jnp.dot(p.astype(BF16), v, preferred_element_type=F32)
        return m_new, l_new, acc

    init = (jnp.full((tq, 1), -1e30, F32), jnp.zeros((tq, 1), F32), jnp.zeros((tq, V_DIM), F32))
    carry = step(0, n_ctx, init)
    trips = jnp.where(qi * tq >= n_ctx, n_kx, 0)
    carry = lax.fori_loop(
        0, trips, lambda j, c: step(pl.multiple_of(n_ctx + j * tk, math.gcd(n_ctx, tk)), tk, c), carry)
    _, l_fin, acc = carry
    o_ref[...] = (acc / l_fin * _silu(mg_ref[...].astype(F32))).astype(BF16)


def _attention(q, k, v, p, n_ctx, mg_off):
    heads, m, _ = q.shape
    tq = _pick(n_ctx, (256, 128))
    tk = _pick(m - n_ctx, (1024, 512, 256))
    mg_blk = mg_off // V_DIM
    return pl.pallas_call(
        functools.partial(_attn_kernel, tq=tq, tk=tk, n_ctx=n_ctx, n_kx=(m - n_ctx) // tk),
        grid=(heads, m // tq),
        in_specs=[
            pl.BlockSpec((None, tq, QK_DIM), lambda h, i: (h, i, 0)),
            pl.BlockSpec((None, m, QK_DIM), lambda h, i: (h, 0, 0)),
            pl.BlockSpec((None, m, V_DIM), lambda h, i: (h, 0, 0)),
            pl.BlockSpec((tq, V_DIM), lambda h, i: (i, mg_blk + h)),
        ],
        out_specs=pl.BlockSpec((tq, V_DIM), lambda h, i: (i, h)),
        out_shape=jax.ShapeDtypeStruct((m, heads * V_DIM), BF16),
        compiler_params=_cparams("parallel", "parallel"),
        name="attention",
    )(q, k, v, p)


def _s5_kernel(u_ref, bdb_ref, coef_ref, bdc_ref, y_ref, h_ref, carry_ref, *, tb, reverse):
    @pl.when(pl.program_id(1) == 0)
    def _():
        carry_ref[...] = jnp.zeros_like(carry_ref)

    ns = SLAB_STATES
    h_ref[...] = jnp.dot(u_ref[...], bdb_ref[...], preferred_element_type=F32)
    n_tiles = tb // SUBLANES

    def tile(i, carry):
        c_re, c_im = carry
        idx = (n_tiles - 1 - i) if reverse else i
        r0 = pl.multiple_of(idx * SUBLANES, SUBLANES)
        re = h_ref[pl.ds(r0, SUBLANES), 0:ns]
        im = h_ref[pl.ds(r0, SUBLANES), ns:2 * ns]
        for s, d in enumerate((1, 2, 4)):
            sh = (SUBLANES - d) if reverse else d
            s_re = pltpu.roll(re, sh, 0)
            s_im = pltpu.roll(im, sh, 0)
            a = coef_ref[2 * s]
            b = coef_ref[2 * s + 1]
            re, im = re + (a * s_re - b * s_im), im + (a * s_im + b * s_re)
        p_re = coef_ref[6]
        p_im = coef_ref[7]
        re, im = re + (p_re * c_re - p_im * c_im), im + (p_re * c_im + p_im * c_re)
        h_ref[pl.ds(r0, SUBLANES), 0:ns] = re
        h_ref[pl.ds(r0, SUBLANES), ns:2 * ns] = im
        last = 0 if reverse else SUBLANES - 1
        return re[last:last + 1, :], im[last:last + 1, :]

    c_re, c_im = lax.fori_loop(0, n_tiles, tile, (carry_ref[0:1, :], carry_ref[1:2, :]))
    carry_ref[0:1, :] = c_re
    carry_ref[1:2, :] = c_im
    y_ref[...] = jnp.dot(h_ref[...].astype(BF16), bdc_ref[...], preferred_element_type=F32)


def _s5_scan(p, bdb, coef, bdc, n_ctx, su_off, ssm_w, reverse):
    m = p.shape[0]
    tb = _pick(n_ctx, (256, 128))
    nb = m // tb
    nc = n_ctx // tb
    slabs = ssm_w // SLAB_W
    u_blk = su_off // SLAB_W

    if reverse:
        def blk(t):
            return jnp.where(t < nc, nc - 1 - t, nb - 1 - (t - nc))
    else:
        def blk(t):
            return t

    return pl.pallas_call(
        functools.partial(_s5_kernel, tb=tb, reverse=reverse),
        grid=(slabs, nb),
        in_specs=[
            pl.BlockSpec((tb, SLAB_W), lambda r, t: (blk(t), u_blk + r)),
            pl.BlockSpec((None, SLAB_W, 2 * SLAB_STATES), lambda r, t: (r, 0, 0)),
            pl.BlockSpec((None, 8, SUBLANES, SLAB_STATES), lambda r, t: (r, 0, 0, 0)),
            pl.BlockSpec((None, 2 * SLAB_STATES, SLAB_W), lambda r, t: (r, 0, 0)),
        ],
        out_specs=pl.BlockSpec((tb, SLAB_W), lambda r, t: (blk(t), r)),
        out_shape=jax.ShapeDtypeStruct((m, ssm_w), F32),
        scratch_shapes=[pltpu.VMEM((tb, 2 * SLAB_STATES), F32), pltpu.VMEM((SUBLANES, SLAB_STATES), F32)],
        compiler_params=_cparams("parallel", "arbitrary"),
        name="s5_bwd" if reverse else "s5_fwd",
    )(p, bdb, coef, bdc)


def _s5_params(lam_re, lam_im, log_dt, b_re, b_im, c_re, c_im, reverse):
    g = lam_re.shape[0]
    slabs = g // SLAB_GROUPS
    lam = lax.complex(lam_re.astype(F32), lam_im.astype(F32))
    dt = jnp.exp(log_dt.astype(F32))[:, None]
    lam_bar = jnp.exp(lam * dt)
    b_bar = ((lam_bar - 1.0) / lam)[..., None] * lax.complex(b_re.astype(F32), b_im.astype(F32))
    eye = jnp.eye(SLAB_GROUPS, dtype=F32)

    def bd_in(w):
        w = w.reshape(slabs, SLAB_GROUPS, STATE_N, GROUP_CH)
        return jnp.einsum("rgnc,gh->rgchn", w, eye).reshape(slabs, SLAB_W, SLAB_STATES)

    def bd_out(w):
        w = w.reshape(slabs, SLAB_GROUPS, GROUP_CH, STATE_N)
        return jnp.einsum("rgcn,gh->rgnhc", w, eye).reshape(slabs, SLAB_STATES, SLAB_W)

    bdb = jnp.concatenate([bd_in(jnp.real(b_bar)), bd_in(jnp.imag(b_bar))], axis=2).astype(BF16)
    bdc = jnp.concatenate([bd_out(c_re.astype(F32)), bd_out(-c_im.astype(F32))], axis=1).astype(BF16)

    rows = jnp.arange(SUBLANES)
    lam_dt = (lam * dt).reshape(slabs, 1, SLAB_STATES)

    def power(k):
        return jnp.exp(lam_dt * k.astype(F32)[None, :, None])

    coefs = []
    for d in (1, 2, 4):
        keep = (rows < SUBLANES - d) if reverse else (rows >= d)
        pw = power(jnp.full((SUBLANES,), d)) * keep[None, :, None]
        coefs += [jnp.real(pw), jnp.imag(pw)]
    pw = power((SUBLANES - rows) if reverse else (rows + 1))
    coefs += [jnp.real(pw), jnp.imag(pw)]
    coef = jnp.stack(coefs, axis=1).astype(F32)
    return bdb, coef, bdc


def _glu_kernel(u_ref, yf_ref, yb_ref, d_ref, w_ref, b_ref, sg_ref, o_ref):
    y = d_ref[...] * u_ref[...].astype(F32) + yf_ref[...] + yb_ref[...]
    z = jax.nn.gelu(y)
    g = jnp.dot(z.astype(BF16), w_ref[...], preferred_element_type=F32) + b_ref[...]
    o_ref[...] = (z * jax.nn.sigmoid(g) * _silu(sg_ref[...].astype(F32))).astype(BF16)


def _s5_glu(p, yf, yb, d_skip, w_glu, b_glu, su_off, sg_off):
    m = p.shape[0]
    w = yf.shape[1]
    tm = _pick(m, (256, 128))
    row = pl.BlockSpec((tm, w), lambda i: (i, 0))
    vec = pl.BlockSpec((1, w), lambda i: (0, 0))
    return pl.pallas_call(
        _glu_kernel,
        grid=(m // tm,),
        in_specs=[
            pl.BlockSpec((tm, w), lambda i: (i, su_off // w)), row, row, vec,
            pl.BlockSpec((w, w), lambda i: (0, 0)), vec,
            pl.BlockSpec((tm, w), lambda i: (i, sg_off // w)),
        ],
        out_specs=row,
        out_shape=jax.ShapeDtypeStruct((m, w), BF16),
        compiler_params=_cparams("parallel"),
        name="s5_glu",
    )(p, yf, yb, d_skip, w_glu, b_glu, p)


def _conv_kernel(a_ref, b_ref, ap_ref, bp_ref, an_ref, bn_ref, wdw_ref, bdw_ref, lng_ref, lnb_ref,
                 wpw_ref, bpw_ref, cg_ref, o_ref, u_ref, c_ref, *, tm, n_ctx, n_rows):
    i = pl.program_id(0)
    row0 = i * tm
    w = u_ref.shape[1]
    has_prev = jnp.logical_and(row0 != 0, row0 != n_ctx)
    has_next = jnp.logical_and(row0 + tm != n_ctx, row0 + tm != n_rows)

    def glu(a, b):
        return a[...].astype(F32) * jax.nn.sigmoid(b[...].astype(F32))

    u_ref[0:CONV_HALO, :] = jnp.where(has_prev, glu(ap_ref, bp_ref), 0.0)
    u_ref[CONV_HALO:CONV_HALO + tm, :] = glu(a_ref, b_ref)
    u_ref[CONV_HALO + tm:2 * CONV_HALO + tm, :] = jnp.where(has_next, glu(an_ref, bn_ref), 0.0)

    rc = 32
    lc = _pick(w, (512, 256))
    first = CONV_HALO - CONV_K // 2

    def rows(ci, _):
        r = pl.multiple_of(ci * rc, rc)
        for l in range(0, w, lc):
            win = u_ref[pl.ds(r, rc + 2 * CONV_HALO), l:l + lc]
            acc = jnp.zeros((rc, lc), F32)
            for k in range(CONV_K):
                tap = pltpu.roll(win, rc + 2 * CONV_HALO - (first + k), 0)[0:rc, :]
                acc = acc + wdw_ref[k:k + 1, l:l + lc] * tap
            c_ref[pl.ds(r, rc), l:l + lc] = acc + bdw_ref[:, l:l + lc]
        return 0

    lax.fori_loop(0, tm // rc, rows, 0)

    c = c_ref[...]
    mu = jnp.mean(c, axis=-1, keepdims=True)
    var = jnp.mean(jnp.square(c - mu), axis=-1, keepdims=True)
    y = _silu((c - mu) * lax.rsqrt(var + NORM_EPS) * lng_ref[...] + lnb_ref[...])
    out = jnp.dot(y.astype(BF16), wpw_ref[...], preferred_element_type=F32) + bpw_ref[...]
    o_ref[...] = (out * _silu(cg_ref[...].astype(F32))).astype(BF16)


def _conv_module(p, w_dw, b_dw, ln_g, ln_b, w_pw, b_pw, n_ctx, ca_off, cb_off, cg_off):
    m = p.shape[0]
    w = w_pw.shape[0]
    tm = _pick(n_ctx, (256, 128))
    hb = tm // CONV_HALO
    n_halo = m // CONV_HALO
    ca, cb, cg = ca_off // w, cb_off // w, cg_off // w

    def prev(i):
        return jnp.maximum(i * hb - 1, 0)

    def nxt(i):
        return jnp.minimum((i + 1) * hb, n_halo - 1)

    vec = pl.BlockSpec((1, w), lambda i: (0, 0))
    return pl.pallas_call(
        functools.partial(_conv_kernel, tm=tm, n_ctx=n_ctx, n_rows=m),
        grid=(m // tm,),
        in_specs=[
            pl.BlockSpec((tm, w), lambda i: (i, ca)),
            pl.BlockSpec((tm, w), lambda i: (i, cb)),
            pl.BlockSpec((CONV_HALO, w), lambda i: (prev(i), ca)),
            pl.BlockSpec((CONV_HALO, w), lambda i: (prev(i), cb)),
            pl.BlockSpec((CONV_HALO, w), lambda i: (nxt(i), ca)),
            pl.BlockSpec((CONV_HALO, w), lambda i: (nxt(i), cb)),
            pl.BlockSpec((CONV_K, w), lambda i: (0, 0)),
            vec, vec, vec,
            pl.BlockSpec((w, w), lambda i: (0, 0)),
            vec,
            pl.BlockSpec((tm, w), lambda i: (i, cg)),
        ],
        out_specs=pl.BlockSpec((tm, w), lambda i: (i, 0)),
        out_shape=jax.ShapeDtypeStruct((m, w), BF16),
        scratch_shapes=[pltpu.VMEM((tm + 2 * CONV_HALO, w), F32), pltpu.VMEM((tm, w), F32)],
        compiler_params=_cparams("parallel"),
        name="conv_module",
    )(p, p, p, p, p, p, w_dw, b_dw, ln_g, ln_b, w_pw, b_pw, p)


def _merge_kernel(a_ref, s_ref, c_ref, wm_ref, ws_ref, wc_ref, gm_ref, gs_ref, gc_ref, o_ref):
    def branch(x_ref, w_ref, g_ref):
        y = jnp.dot(x_ref[...], w_ref[...], preferred_element_type=F32)
        return jax.nn.sigmoid(g_ref[...].astype(F32)) * y

    o_ref[...] = (branch(a_ref, wm_ref, gm_ref) + branch(s_ref, ws_ref, gs_ref)
                  + branch(c_ref, wc_ref, gc_ref)).astype(BF16)


def _merge(o_mla, o_ssm, o_conv, w_bm, w_bs, w_bc, p, gm_off, gs_off, gc_off):
    m = p.shape[0]
    d = w_bm.shape[1]
    tm = _pick(m, (768, 640, 512, 256))
    tn = _pick(d, (512, 256))

    def lhs(x):
        return pl.BlockSpec((tm, x.shape[1]), lambda i, j: (i, 0))

    def wgt(x):
        return pl.BlockSpec((x.shape[0], tn), lambda i, j: (0, j))

    def gate(off):
        return pl.BlockSpec((tm, tn), lambda i, j: (i, off // tn + j))

    return pl.pallas_call(
        _merge_kernel,
        grid=(m // tm, d // tn),
        in_specs=[lhs(o_mla), lhs(o_ssm), lhs(o_conv), wgt(w_bm), wgt(w_bs), wgt(w_bc),
                  gate(gm_off), gate(gs_off), gate(gc_off)],
        out_specs=pl.BlockSpec((tm, tn), lambda i, j: (i, j)),
        out_shape=jax.ShapeDtypeStruct((m, d), BF16),
        compiler_params=_cparams("parallel", "arbitrary"),
        name="merge",
    )(o_mla, o_ssm, o_conv, w_bm, w_bs, w_bc, p, p, p)


def _outproj_kernel(m_ref, w_ref, z_ref, g_ref, gate_ref, o_ref, acc_ref, *, tm, tn, nj, n_ctx):
    i = pl.program_id(0)
    j = pl.program_id(1)
    acc_ref[j] = jnp.dot(m_ref[...], w_ref[...], preferred_element_type=F32)

    @pl.when(j == nj - 1)
    def _():
        ss = jnp.zeros((tm, 1), F32)
        for jj in range(nj):
            t = acc_ref[jj]
            ss = ss + jnp.sum(t * t, axis=-1, keepdims=True)
        inv = lax.rsqrt(ss / (nj * tn) + NORM_EPS)
        for jj in range(nj):
            sl = slice(jj * tn, (jj + 1) * tn)
            gate = _row_select(i, tm, n_ctx, gate_ref[0:1, sl], gate_ref[1:2, sl])
            o_ref[:, sl] = z_ref[:, sl] + gate * (acc_ref[jj] * inv * g_ref[:, sl])


def _outproj(merged, w_out, z, g_post, gate, n_ctx):
    m, d = z.shape
    tm = _pick(m, (384, 256, 128))
    tn = _pick(d, (512, 256))
    nj = d // tn
    return pl.pallas_call(
        functools.partial(_outproj_kernel, tm=tm, tn=tn, nj=nj, n_ctx=n_ctx),
        grid=(m // tm, nj),
        in_specs=[
            pl.BlockSpec((tm, d), lambda i, j: (i, 0)),
            pl.BlockSpec((d, tn), lambda i, j: (0, j)),
            pl.BlockSpec((tm, d), lambda i, j: (i, 0)),
            pl.BlockSpec((1, d), lambda i, j: (0, 0)),
            pl.BlockSpec((SUBLANES, d), lambda i, j: (0, 0)),
        ],
        out_specs=pl.BlockSpec((tm, d), lambda i, j: (i, 0)),
        out_shape=jax.ShapeDtypeStruct((m, d), F32),
        scratch_shapes=[pltpu.VMEM((nj, tm, tn), F32)],
        compiler_params=_cparams("parallel", "arbitrary"),
        name="outproj",
    )(merged, w_out, z, g_post, gate)


def _rot_cols(w):
    q = ROPE_DIM // 4
    return jnp.concatenate([-w[..., q:2 * q], w[..., 0:q], -w[..., 3 * q:4 * q], w[..., 2 * q:3 * q]], axis=-1)


def _rope_tables(n_ctx, seq):
    rows = seq // GRID_W
    row = jnp.repeat(jnp.arange(rows, dtype=F32), GRID_W)
    col = jnp.tile(jnp.arange(GRID_W, dtype=F32), rows)
    half = ROPE_DIM // 2
    inv = 1.0 / (ROPE_BASE ** (jnp.arange(0, half, 2, dtype=F32) / half))
    a0 = row[:, None] * inv
    a1 = col[:, None] * inv
    ang = jnp.concatenate([a0, a0, a1, a1], axis=1)
    cos = jnp.concatenate([jnp.ones((n_ctx, ROPE_DIM), F32), jnp.cos(ang)], axis=0)
    sin = jnp.concatenate([jnp.zeros((n_ctx, ROPE_DIM), F32), jnp.sin(ang)], axis=0)
    pad = jnp.zeros((n_ctx + seq, LANES - ROPE_DIM), F32)
    return jnp.concatenate([cos, pad], axis=1), jnp.concatenate([sin, pad], axis=1)


def kernel(x, c, ctx, c_ctx, w_mod, b_mod, g_pre, g_post, w_in, q_norm, kv_norm, w_uq, w_ukv, lam_re, lam_im, log_dt, b_re, b_im, c_re, c_im, d_skip, w_glu, b_glu, w_dw, b_dw, ln_g, ln_b, w_pw, b_pw, w_bm, w_bs, w_bc, w_out):
    batch, seq, d = x.shape
    assert batch == 1, "kernel handles the problem's batch of one"
    n_ctx = ctx.shape[1]
    depth = w_in.shape[0]
    q_rank = q_norm.shape[1]
    kv_rank = kv_norm.shape[1]
    ssm_w = d_skip.shape[1]
    conv_w = b_dw.shape[1]
    heads = w_uq.shape[2] // QK_DIM
    mla_w = heads * V_DIM
    assert ssm_w == conv_w == mla_w and q_rank % kv_rank == 0 and ssm_w % SLAB_W == 0

    head_w = q_rank + kv_rank + 2 * ROPE_DIM
    su_off = -(-head_w // ssm_w) * ssm_w
    mg_off = su_off + ssm_w
    sg_off = mg_off + mla_w
    ca_off = sg_off + ssm_w
    cb_off = ca_off + conv_w
    cg_off = cb_off + conv_w
    gm_off = cg_off + conv_w
    gs_off = gm_off + d
    gc_off = gs_off + d
    o_kv, o_kr = 0, kv_rank
    o_su = o_kr + ROPE_DIM
    o_q = o_su + ssm_w
    o_rest = o_q + q_rank

    z = jnp.concatenate([ctx[0], x[0]], axis=0)
    cvec = jnp.concatenate([c_ctx[None, :], c, jnp.zeros((SUBLANES - 1 - batch, d), F32)], axis=0)
    cos, sin = _rope_tables(n_ctx, seq)
    b_mod3 = b_mod[:, None, :]

    for l in range(depth):
        mod = _modulation(cvec, w_mod, b_mod3, l)
        shift, scale, gate = mod[:, 0:d], mod[:, d:2 * d], mod[:, 2 * d:3 * d]
        h = _prenorm(z, g_pre[l][None, :], shift, scale, n_ctx)

        wl = w_in[l]
        w_kr = wl[:, o_kr:o_kr + ROPE_DIM]
        w_pack = jnp.concatenate([
            wl[:, o_q:o_q + q_rank], wl[:, o_kv:o_kv + kv_rank], w_kr, _rot_cols(w_kr),
            jnp.zeros((d, su_off - head_w), F32), wl[:, o_su:o_su + ssm_w], wl[:, o_rest:]], axis=1).astype(BF16)
        p = _inproj(h, w_pack)

        wq = w_uq[l].reshape(q_rank, heads, QK_DIM)
        wq = jnp.concatenate([wq, _rot_cols(wq[..., NOPE_DIM:])], axis=-1).transpose(1, 0, 2).astype(BF16)
        wkv = w_ukv[l].reshape(kv_rank, heads, NOPE_DIM + V_DIM).transpose(1, 0, 2).astype(BF16)
        q = _qup(p, q_norm[l][None, :], wq, cos, sin, q_rank)
        k, v = _kvup(p, kv_norm[l][None, :], wkv, cos, sin, q_rank, kv_rank)
        o_mla = _attention(q, k, v, p, n_ctx, mg_off)

        ys = []
        for direction in range(2):
            bdb, coef, bdc = _s5_params(lam_re[l, direction], lam_im[l, direction], log_dt[l, direction],
                                        b_re[l, direction], b_im[l, direction], c_re[l, direction],
                                        c_im[l, direction], reverse=direction == 1)
            ys.append(_s5_scan(p, bdb, coef, bdc, n_ctx, su_off, ssm_w, reverse=direction == 1))
        o_ssm = _s5_glu(p, ys[0], ys[1], d_skip[l][None, :], w_glu[l].astype(BF16), b_glu[l][None, :],
                        su_off, sg_off)

        o_conv = _conv_module(p, w_dw[l], b_dw[l][None, :], ln_g[l][None, :], ln_b[l][None, :],
                              w_pw[l].astype(BF16), b_pw[l][None, :], n_ctx, ca_off, cb_off, cg_off)

        merged = _merge(o_mla, o_ssm, o_conv, w_bm[l].astype(BF16), w_bs[l].astype(BF16), w_bc[l].astype(BF16),
                        p, gm_off, gs_off, gc_off)
        z = _outproj(merged, w_out[l].astype(BF16), z, g_post[l][None, :], gate, n_ctx)

    return z[n_ctx:][None]
```

```python
import functools
import math

import jax
import jax.numpy as jnp
from jax import lax
from jax.experimental import pallas as pl
from jax.experimental.pallas import tpu as pltpu

NOPE_DIM = 128
ROPE_DIM = 64
V_DIM = 128
QK_DIM = NOPE_DIM + ROPE_DIM
ROPE_BASE = 10000.0
GRID_W = 64
ATTN_SCALE = 1.0 / math.sqrt(NOPE_DIM + ROPE_DIM)
LOG2E = math.log2(math.e)
GROUP_CH = 16
STATE_N = 64
CONV_K = 31
CONV_HALO = 16
NORM_EPS = 1e-6

LANES = 128
SUBLANES = 8
SLAB_GROUPS = 16
SLAB_W = SLAB_GROUPS * GROUP_CH
SLAB_STATES = SLAB_GROUPS * STATE_N
VMEM_LIMIT = 56 * 1024 * 1024

F32 = jnp.float32
BF16 = jnp.bfloat16
NT_DIMS = (((1,), (1,)), ((), ()))


def _pick(n, cands):
    for c in cands:
        if n % c == 0:
            return c
    raise ValueError(f"no tile for {n} in {cands}")


def _cparams(*sem):
    return pltpu.CompilerParams(dimension_semantics=sem, vmem_limit_bytes=VMEM_LIMIT)


def _silu(v):
    return v * jax.nn.sigmoid(v)


def _rms(v, g):
    return v * lax.rsqrt(jnp.mean(v * v, axis=-1, keepdims=True) + NORM_EPS) * g


def _row_select(i, tm, seq, x_row, ctx_row):
    rows = i * tm + lax.broadcasted_iota(jnp.int32, (tm, 1), 0)
    return jnp.where(rows < seq, x_row, ctx_row)


def _mod_kernel(c_ref, w_ref, b_ref, o_ref):
    a = _silu(c_ref[...])
    o_ref[...] = jnp.dot(a.astype(BF16), w_ref[...].astype(BF16), preferred_element_type=F32) + b_ref[...]


def _modulation(cvec, w_mod, b_mod3, layer):
    d = cvec.shape[1]
    n = w_mod.shape[2]
    tn = _pick(n, (512, 256, 128))
    return pl.pallas_call(
        _mod_kernel,
        grid=(n // tn,),
        in_specs=[
            pl.BlockSpec((SUBLANES, d), lambda j: (0, 0)),
            pl.BlockSpec((None, d, tn), lambda j: (layer, 0, j)),
            pl.BlockSpec((None, 1, tn), lambda j: (layer, 0, j)),
        ],
        out_specs=pl.BlockSpec((SUBLANES, tn), lambda j: (0, j)),
        out_shape=jax.ShapeDtypeStruct((SUBLANES, n), F32),
        compiler_params=_cparams("parallel"),
        name="modulation",
    )(cvec, w_mod, b_mod3)


def _prenorm_kernel(z_ref, g_ref, sh_ref, sc_ref, o_ref, *, tm, seq):
    i = pl.program_id(0)
    y = _rms(z_ref[...], g_ref[...])
    shift = _row_select(i, tm, seq, sh_ref[0:1, :], sh_ref[1:2, :])
    scale = _row_select(i, tm, seq, sc_ref[0:1, :], sc_ref[1:2, :])
    o_ref[...] = (y * (1.0 + scale) + shift).astype(BF16)


def _prenorm(z, g, shift, scale, seq):
    m, d = z.shape
    tm = _pick(m, (256, 128))
    vec = pl.BlockSpec((SUBLANES, d), lambda i: (0, 0))
    return pl.pallas_call(
        functools.partial(_prenorm_kernel, tm=tm, seq=seq),
        grid=(m // tm,),
        in_specs=[pl.BlockSpec((tm, d), lambda i: (i, 0)), pl.BlockSpec((1, d), lambda i: (0, 0)), vec, vec],
        out_specs=pl.BlockSpec((tm, d), lambda i: (i, 0)),
        out_shape=jax.ShapeDtypeStruct((m, d), BF16),
        compiler_params=_cparams("parallel"),
        name="prenorm",
    )(z, g, shift, scale)


def _inproj_head_kernel(h_ref, w_ref, o_ref, wb_ref):
    @pl.when(pl.program_id(0) == 0)
    def _():
        wb_ref[...] = w_ref[...].astype(BF16)

    o_ref[...] = jnp.dot(h_ref[...], wb_ref[...], preferred_element_type=F32).astype(BF16)


def _inproj_head(h, w_in, layer, head_w):
    m, k = h.shape
    tm = _pick(m, (768, 640, 512, 256))
    return pl.pallas_call(
        _inproj_head_kernel,
        grid=(m // tm,),
        in_specs=[pl.BlockSpec((tm, k), lambda i: (i, 0)), pl.BlockSpec((None, k, head_w), lambda i: (layer, 0, 0))],
        out_specs=pl.BlockSpec((tm, head_w), lambda i: (i, 0)),
        out_shape=jax.ShapeDtypeStruct((m, head_w), BF16),
        scratch_shapes=[pltpu.VMEM((k, head_w), BF16)],
        compiler_params=_cparams("arbitrary"),
        name="inproj_head",
    )(h, w_in)


def _inproj_body_kernel(h_ref, wa_ref, wn_ref, o_ref, wb_ref, *, shift, tn, rc):
    @pl.when(pl.program_id(1) == 0)
    def _():
        def rows(ci, _):
            r = pl.multiple_of(ci * rc, rc)
            both = jnp.concatenate([wa_ref[pl.ds(r, rc), :], wn_ref[pl.ds(r, rc), :]], axis=1)
            wb_ref[pl.ds(r, rc), :] = pltpu.roll(both, tn + LANES - shift, 1)[:, 0:tn].astype(BF16)
            return 0

        lax.fori_loop(0, wa_ref.shape[0] // rc, rows, 0)

    o_ref[...] = jnp.dot(h_ref[...], wb_ref[...], preferred_element_type=F32).astype(BF16)


def _inproj_body(h, w_in, layer, first_col, n_su, n_q, n_rest, tn):
    m, k = h.shape
    tm = _pick(m, (768, 640, 512, 256))
    shift = first_col % LANES
    base = (first_col - shift) // tn
    nj = n_su + n_q + n_rest

    def src(j):
        return jnp.where(j < n_su, j, jnp.where(j < n_su + n_rest, j + n_q, j - n_rest))

    return pl.pallas_call(
        functools.partial(_inproj_body_kernel, shift=shift, tn=tn, rc=64),
        grid=(nj, m // tm),
        in_specs=[
            pl.BlockSpec((tm, k), lambda j, i: (i, 0)),
            pl.BlockSpec((None, k, tn), lambda j, i: (layer, 0, base + src(j))),
            pl.BlockSpec((None, k, LANES), lambda j, i: (layer, 0, (base + src(j) + 1) * (tn // LANES))),
        ],
        out_specs=pl.BlockSpec((tm, tn), lambda j, i: (i, j)),
        out_shape=jax.ShapeDtypeStruct((m, nj * tn), BF16),
        scratch_shapes=[pltpu.VMEM((k, tn), BF16)],
        compiler_params=_cparams("parallel", "arbitrary"),
        name="inproj_body",
    )(h, w_in, w_in)


def _qup_kernel(p_ref, g_ref, w_ref, cos_ref, sin_ref, o_ref, qn_ref):
    @pl.when(pl.program_id(1) == 0)
    def _():
        qn_ref[...] = _rms(p_ref[...].astype(F32), g_ref[...]).astype(BF16)

    y = jnp.dot(qn_ref[...], w_ref[...], preferred_element_type=F32)
    y2 = y[:, NOPE_DIM:]
    r = y2 * cos_ref[...] + pltpu.roll(y2, ROPE_DIM, 1) * sin_ref[...]
    scale = ATTN_SCALE * LOG2E
    o_ref[:, 0:NOPE_DIM] = (y[:, 0:NOPE_DIM] * scale).astype(BF16)
    o_ref[:, NOPE_DIM:QK_DIM] = (r[:, 0:ROPE_DIM] * scale).astype(BF16)


def _qup(p, g, wq, cos, sin, q_off):
    m = p.shape[0]
    heads, q_rank, _ = wq.shape
    tm = _pick(m, (768, 640, 512, 256))
    return pl.pallas_call(
        _qup_kernel,
        grid=(m // tm, heads),
        in_specs=[
            pl.BlockSpec((tm, q_rank), lambda i, h: (i, q_off // q_rank)),
            pl.BlockSpec((1, q_rank), lambda i, h: (0, 0)),
            pl.BlockSpec((None, q_rank, 2 * LANES), lambda i, h: (h, 0, 0)),
            pl.BlockSpec((tm, LANES), lambda i, h: (i, 0)),
            pl.BlockSpec((tm, LANES), lambda i, h: (i, 0)),
        ],
        out_specs=pl.BlockSpec((None, tm, QK_DIM), lambda i, h: (h, i, 0)),
        out_shape=jax.ShapeDtypeStruct((heads, m, QK_DIM), BF16),
        scratch_shapes=[pltpu.VMEM((tm, q_rank), BF16)],
        compiler_params=_cparams("parallel", "arbitrary"),
        name="q_up",
    )(p, g, wq, cos, sin)


def _kvup_kernel(p_ref, kr_ref, g_ref, wk_ref, wvt_ref, cos_ref, sin_ref, k_ref, vt_ref, kvn_ref, krr_ref):
    @pl.when(pl.program_id(1) == 0)
    def _():
        kvn_ref[...] = _rms(p_ref[...].astype(F32), g_ref[...]).astype(BF16)
        kr = kr_ref[...].astype(F32)
        quarter = ROPE_DIM // 4
        lane = lax.broadcasted_iota(jnp.int32, kr.shape, 1)
        rot = jnp.where(lane % (2 * quarter) < quarter,
                        -pltpu.roll(kr, LANES - quarter, 1), pltpu.roll(kr, quarter, 1))
        krr_ref[...] = (kr * cos_ref[...] + rot * sin_ref[...]).astype(BF16)

    kvn = kvn_ref[...]
    k_ref[:, 0:NOPE_DIM] = jnp.dot(kvn, wk_ref[...], preferred_element_type=F32).astype(BF16)
    k_ref[:, NOPE_DIM:QK_DIM] = krr_ref[:, 0:ROPE_DIM]
    vt_ref[...] = lax.dot_general(wvt_ref[...], kvn, NT_DIMS, preferred_element_type=F32).astype(BF16)


def _kvup(p, g, wk, wvt, cos, sin):
    m = p.shape[0]
    heads, kv_rank, _ = wk.shape
    tm = _pick(m, (768, 640, 512, 256))
    return pl.pallas_call(
        _kvup_kernel,
        grid=(m // tm, heads),
        in_specs=[
            pl.BlockSpec((tm, kv_rank), lambda i, h: (i, 0)),
            pl.BlockSpec((tm, LANES), lambda i, h: (i, kv_rank // LANES)),
            pl.BlockSpec((1, kv_rank), lambda i, h: (0, 0)),
            pl.BlockSpec((None, kv_rank, NOPE_DIM), lambda i, h: (h, 0, 0)),
            pl.BlockSpec((None, V_DIM, kv_rank), lambda i, h: (h, 0, 0)),
            pl.BlockSpec((tm, LANES), lambda i, h: (i, 0)),
            pl.BlockSpec((tm, LANES), lambda i, h: (i, 0)),
        ],
        out_specs=[
            pl.BlockSpec((None, tm, QK_DIM), lambda i, h: (h, i, 0)),
            pl.BlockSpec((None, V_DIM, tm), lambda i, h: (h, 0, i)),
        ],
        out_shape=[
            jax.ShapeDtypeStruct((heads, m, QK_DIM), BF16),
            jax.ShapeDtypeStruct((heads, V_DIM, m), BF16),
        ],
        scratch_shapes=[pltpu.VMEM((tm, kv_rank), BF16), pltpu.VMEM((tm, LANES), BF16)],
        compiler_params=_cparams("parallel", "arbitrary"),
        name="kv_up",
    )(p, p, g, wk, wvt, cos, sin)


def _attn_kernel(q_ref, k_ref, vt_ref, mg_ref, o_ref, *, tq, chunks):
    q = q_ref[...]
    m_run = jnp.full((1, tq), -1e30, F32)
    l_run = jnp.zeros((1, tq), F32)
    acc = jnp.zeros((V_DIM, tq), F32)
    for off, size in chunks:
        s = lax.dot_general(k_ref[off:off + size, :], q, NT_DIMS, preferred_element_type=F32)
        m_new = jnp.maximum(m_run, jnp.max(s, axis=0, keepdims=True))
        alpha = jnp.exp2(m_run - m_new)
        p = jnp.exp2(s - m_new)
        l_run = alpha * l_run + jnp.sum(p, axis=0, keepdims=True)
        acc = alpha * acc + jnp.dot(vt_ref[:, off:off + size], p.astype(BF16), preferred_element_type=F32)
        m_run = m_new
    o_ref[...] = ((acc / l_run).T * _silu(mg_ref[...].astype(F32))).astype(BF16)


def _attention(q, k, vt, p, mg_off, q_rows, key_rows, tq, tk):
    heads = q.shape[0]
    q0, nq = q_rows
    k0, nk = key_rows
    assert q0 % tq == 0 and nq % tq == 0 and k0 % nk == 0
    mg_blk = mg_off // V_DIM
    tail = nk % tk
    chunks = ((nk - tail, tail),) if tail else ()
    chunks += tuple((off, tk) for off in range(0, nk - tail, tk))
    return pl.pallas_call(
        functools.partial(_attn_kernel, tq=tq, chunks=chunks),
        grid=(heads, nq // tq),
        in_specs=[
            pl.BlockSpec((None, tq, QK_DIM), lambda h, i: (h, q0 // tq + i, 0)),
            pl.BlockSpec((None, nk, QK_DIM), lambda h, i: (h, k0 // nk, 0)),
            pl.BlockSpec((None, V_DIM, nk), lambda h, i: (h, 0, k0 // nk)),
            pl.BlockSpec((tq, V_DIM), lambda h, i: (q0 // tq + i, mg_blk + h)),
        ],
        out_specs=pl.BlockSpec((tq, V_DIM), lambda h, i: (i, h)),
        out_shape=jax.ShapeDtypeStruct((nq, heads * V_DIM), BF16),
        compiler_params=_cparams("parallel", "parallel"),
        name="attention",
    )(q, k, vt, p)


def _s5_kernel(u_ref, bdb_ref, coef_ref, bdc_ref, y_ref, h_ref, carry_ref, *, tb, reverse):
    @pl.when(pl.program_id(1) == 0)
    def _():
        carry_ref[...] = jnp.zeros_like(carry_ref)

    ns = SLAB_STATES
    h_ref[...] = jnp.dot(u_ref[...], bdb_ref[...], preferred_element_type=F32)
    n_tiles = tb // SUBLANES

    def tile(i, carry):
        c_re, c_im = carry
        idx = (n_tiles - 1 - i) if reverse else i
        r0 = pl.multiple_of(idx * SUBLANES, SUBLANES)
        re = h_ref[pl.ds(r0, SUBLANES), 0:ns]
        im = h_ref[pl.ds(r0, SUBLANES), ns:2 * ns]
        for s, d in enumerate((1, 2, 4)):
            sh = (SUBLANES - d) if reverse else d
            s_re = pltpu.roll(re, sh, 0)
            s_im = pltpu.roll(im, sh, 0)
            a = coef_ref[2 * s]
            b = coef_ref[2 * s + 1]
            re, im = re + (a * s_re - b * s_im), im + (a * s_im + b * s_re)
        p_re = coef_ref[6]
        p_im = coef_ref[7]
        re, im = re + (p_re * c_re - p_im * c_im), im + (p_re * c_im + p_im * c_re)
        h_ref[pl.ds(r0, SUBLANES), 0:ns] = re
        h_ref[pl.ds(r0, SUBLANES), ns:2 * ns] = im
        last = 0 if reverse else SUBLANES - 1
        return re[last:last + 1, :], im[last:last + 1, :]

    c_re, c_im = lax.fori_loop(0, n_tiles, tile, (carry_ref[0:1, :], carry_ref[1:2, :]))
    carry_ref[0:1, :] = c_re
    carry_ref[1:2, :] = c_im
    y_ref[...] = jnp.dot(h_ref[...].astype(BF16), bdc_ref[...], preferred_element_type=F32)


def _s5_scan(p, bdb, coef, bdc, seq, su_off, ssm_w, reverse):
    m = p.shape[0]
    tb = _pick(math.gcd(seq, m - seq), (256, 128))
    nb = m // tb
    nx = seq // tb
    nc = nb - nx
    slabs = ssm_w // SLAB_W
    u_blk = su_off // SLAB_W

    if reverse:
        def blk(t):
            return nb - 1 - t
    else:
        def blk(t):
            return jnp.where(t < nc, nx + t, t - nc)

    return pl.pallas_call(
        functools.partial(_s5_kernel, tb=tb, reverse=reverse),
        grid=(slabs, nb),
        in_specs=[
            pl.BlockSpec((tb, SLAB_W), lambda r, t: (blk(t), u_blk + r)),
            pl.BlockSpec((None, SLAB_W, 2 * SLAB_STATES), lambda r, t: (r, 0, 0)),
            pl.BlockSpec((None, 8, SUBLANES, SLAB_STATES), lambda r, t: (r, 0, 0, 0)),
            pl.BlockSpec((None, 2 * SLAB_STATES, SLAB_W), lambda r, t: (r, 0, 0)),
        ],
        out_specs=pl.BlockSpec((tb, SLAB_W), lambda r, t: (blk(t), r)),
        out_shape=jax.ShapeDtypeStruct((m, ssm_w), F32),
        scratch_shapes=[pltpu.VMEM((tb, 2 * SLAB_STATES), F32), pltpu.VMEM((SUBLANES, SLAB_STATES), F32)],
        compiler_params=_cparams("parallel", "arbitrary"),
        name="s5_bwd" if reverse else "s5_fwd",
    )(p, bdb, coef, bdc)


def _s5_params(lam_re, lam_im, log_dt, b_re, b_im, c_re, c_im, reverse):
    g = lam_re.shape[0]
    slabs = g // SLAB_GROUPS
    lam = lax.complex(lam_re.astype(F32), lam_im.astype(F32))
    dt = jnp.exp(log_dt.astype(F32))[:, None]
    lam_bar = jnp.exp(lam * dt)
    b_bar = ((lam_bar - 1.0) / lam)[..., None] * lax.complex(b_re.astype(F32), b_im.astype(F32))
    eye = jnp.eye(SLAB_GROUPS, dtype=F32)

    def bd_in(w):
        w = w.reshape(slabs, SLAB_GROUPS, STATE_N, GROUP_CH)
        return jnp.einsum("rgnc,gh->rgchn", w, eye).reshape(slabs, SLAB_W, SLAB_STATES)

    def bd_out(w):
        w = w.reshape(slabs, SLAB_GROUPS, GROUP_CH, STATE_N)
        return jnp.einsum("rgcn,gh->rgnhc", w, eye).reshape(slabs, SLAB_STATES, SLAB_W)

    bdb = jnp.concatenate([bd_in(jnp.real(b_bar)), bd_in(jnp.imag(b_bar))], axis=2).astype(BF16)
    bdc = jnp.concatenate([bd_out(c_re.astype(F32)), bd_out(-c_im.astype(F32))], axis=1).astype(BF16)

    rows = jnp.arange(SUBLANES)
    lam_dt = (lam * dt).reshape(slabs, 1, SLAB_STATES)

    def power(k):
        return jnp.exp(lam_dt * k.astype(F32)[None, :, None])

    coefs = []
    for d in (1, 2, 4):
        keep = (rows < SUBLANES - d) if reverse else (rows >= d)
        pw = power(jnp.full((SUBLANES,), d)) * keep[None, :, None]
        coefs += [jnp.real(pw), jnp.imag(pw)]
    pw = power((SUBLANES - rows) if reverse else (rows + 1))
    coefs += [jnp.real(pw), jnp.imag(pw)]
    coef = jnp.stack(coefs, axis=1).astype(F32)
    return bdb, coef, bdc


def _glu_kernel(u_ref, yf_ref, yb_ref, d_ref, w_ref, b_ref, sg_ref, o_ref):
    y = d_ref[...] * u_ref[...].astype(F32) + yf_ref[...] + yb_ref[...]
    z = jax.nn.gelu(y)
    g = jnp.dot(z.astype(BF16), w_ref[...], preferred_element_type=F32) + b_ref[...]
    o_ref[...] = (z * jax.nn.sigmoid(g) * _silu(sg_ref[...].astype(F32))).astype(BF16)


def _s5_glu(p, yf, yb, d_skip, w_glu, b_glu, su_off, sg_off, rows):
    w = yf.shape[1]
    tm = _pick(rows, (256, 128))
    row = pl.BlockSpec((tm, w), lambda i: (i, 0))
    vec = pl.BlockSpec((1, w), lambda i: (0, 0))
    return pl.pallas_call(
        _glu_kernel,
        grid=(rows // tm,),
        in_specs=[
            pl.BlockSpec((tm, w), lambda i: (i, su_off // w)), row, row, vec,
            pl.BlockSpec((w, w), lambda i: (0, 0)), vec,
            pl.BlockSpec((tm, w), lambda i: (i, sg_off // w)),
        ],
        out_specs=row,
        out_shape=jax.ShapeDtypeStruct((rows, w), BF16),
        compiler_params=_cparams("parallel"),
        name="s5_glu",
    )(p, yf, yb, d_skip, w_glu, b_glu, p)


def _conv_kernel(a_ref, b_ref, ap_ref, bp_ref, an_ref, bn_ref, wdw_ref, bdw_ref, lng_ref, lnb_ref,
                 wpw_ref, bpw_ref, cg_ref, o_ref, u_ref, c_ref, *, tm, seq, n_rows):
    i = pl.program_id(0)
    row0 = i * tm
    w = u_ref.shape[1]
    has_prev = jnp.logical_and(row0 != 0, row0 != seq)
    has_next = jnp.logical_and(row0 + tm != seq, row0 + tm != n_rows)

    def glu(a, b):
        return a[...].astype(F32) * jax.nn.sigmoid(b[...].astype(F32))

    u_ref[0:CONV_HALO, :] = jnp.where(has_prev, glu(ap_ref, bp_ref), 0.0)
    u_ref[CONV_HALO:CONV_HALO + tm, :] = glu(a_ref, b_ref)
    u_ref[CONV_HALO + tm:2 * CONV_HALO + tm, :] = jnp.where(has_next, glu(an_ref, bn_ref), 0.0)

    rc = 32
    lc = _pick(w, (512, 256))
    first = CONV_HALO - CONV_K // 2

    def rows(ci, _):
        r = pl.multiple_of(ci * rc, rc)
        for l in range(0, w, lc):
            win = u_ref[pl.ds(r, rc + 2 * CONV_HALO), l:l + lc]
            acc = jnp.zeros((rc, lc), F32)
            for k in range(CONV_K):
                tap = pltpu.roll(win, rc + 2 * CONV_HALO - (first + k), 0)[0:rc, :]
                acc = acc + wdw_ref[k:k + 1, l:l + lc] * tap
            c_ref[pl.ds(r, rc), l:l + lc] = acc + bdw_ref[:, l:l + lc]
        return 0

    lax.fori_loop(0, tm // rc, rows, 0)

    c = c_ref[...]
    mu = jnp.mean(c, axis=-1, keepdims=True)
    var = jnp.mean(jnp.square(c - mu), axis=-1, keepdims=True)
    y = _silu((c - mu) * lax.rsqrt(var + NORM_EPS) * lng_ref[...] + lnb_ref[...])
    out = jnp.dot(y.astype(BF16), wpw_ref[...], preferred_element_type=F32) + bpw_ref[...]
    o_ref[...] = (out * _silu(cg_ref[...].astype(F32))).astype(BF16)


def _conv_module(p, w_dw, b_dw, ln_g, ln_b, w_pw, b_pw, seq, ca_off, cb_off, cg_off, rows):
    m = p.shape[0]
    w = w_pw.shape[0]
    tm = _pick(math.gcd(seq, m - seq), (256, 128))
    hb = tm // CONV_HALO
    n_halo = m // CONV_HALO
    ca, cb, cg = ca_off // w, cb_off // w, cg_off // w

    def prev(i):
        return jnp.maximum(i * hb - 1, 0)

    def nxt(i):
        return jnp.minimum((i + 1) * hb, n_halo - 1)

    vec = pl.BlockSpec((1, w), lambda i: (0, 0))
    return pl.pallas_call(
        functools.partial(_conv_kernel, tm=tm, seq=seq, n_rows=m),
        grid=(rows // tm,),
        in_specs=[
            pl.BlockSpec((tm, w), lambda i: (i, ca)),
            pl.BlockSpec((tm, w), lambda i: (i, cb)),
            pl.BlockSpec((CONV_HALO, w), lambda i: (prev(i), ca)),
            pl.BlockSpec((CONV_HALO, w), lambda i: (prev(i), cb)),
            pl.BlockSpec((CONV_HALO, w), lambda i: (nxt(i), ca)),
            pl.BlockSpec((CONV_HALO, w), lambda i: (nxt(i), cb)),
            pl.BlockSpec((CONV_K, w), lambda i: (0, 0)),
            vec, vec, vec,
            pl.BlockSpec((w, w), lambda i: (0, 0)),
            vec,
            pl.BlockSpec((tm, w), lambda i: (i, cg)),
        ],
        out_specs=pl.BlockSpec((tm, w), lambda i: (i, 0)),
        out_shape=jax.ShapeDtypeStruct((rows, w), BF16),
        scratch_shapes=[pltpu.VMEM((tm + 2 * CONV_HALO, w), F32), pltpu.VMEM((tm, w), F32)],
        compiler_params=_cparams("parallel"),
        name="conv_module",
    )(p, p, p, p, p, p, w_dw, b_dw, ln_g, ln_b, w_pw, b_pw, p)


def _merge_kernel(a_ref, s_ref, c_ref, wm_ref, ws_ref, wc_ref, gm_ref, gs_ref, gc_ref, o_ref):
    def branch(x_ref, w_ref, g_ref):
        y = jnp.dot(x_ref[...], w_ref[...], preferred_element_type=F32)
        return jax.nn.sigmoid(g_ref[...].astype(F32)) * y

    o_ref[...] = (branch(a_ref, wm_ref, gm_ref) + branch(s_ref, ws_ref, gs_ref)
                  + branch(c_ref, wc_ref, gc_ref)).astype(BF16)


def _merge(o_mla, o_ssm, o_conv, w_bm, w_bs, w_bc, p, gm_off, gs_off, gc_off):
    rows = o_mla.shape[0]
    d = w_bm.shape[1]
    tm = _pick(rows, (768, 640, 512, 256))
    tn = _pick(d, (512, 256))

    def lhs(x):
        return pl.BlockSpec((tm, x.shape[1]), lambda i, j: (i, 0))

    def wgt(x):
        return pl.BlockSpec((x.shape[0], tn), lambda i, j: (0, j))

    def gate(off):
        return pl.BlockSpec((tm, tn), lambda i, j: (i, off // tn + j))

    return pl.pallas_call(
        _merge_kernel,
        grid=(rows // tm, d // tn),
        in_specs=[lhs(o_mla), lhs(o_ssm), lhs(o_conv), wgt(w_bm), wgt(w_bs), wgt(w_bc),
                  gate(gm_off), gate(gs_off), gate(gc_off)],
        out_specs=pl.BlockSpec((tm, tn), lambda i, j: (i, j)),
        out_shape=jax.ShapeDtypeStruct((rows, d), BF16),
        compiler_params=_cparams("parallel", "arbitrary"),
        name="merge",
    )(o_mla, o_ssm, o_conv, w_bm, w_bs, w_bc, p, p, p)


def _outproj_kernel(m_ref, w_ref, z_ref, g_ref, gate_ref, o_ref, acc_ref, *, tm, tn, nj, seq):
    i = pl.program_id(0)
    j = pl.program_id(1)
    acc_ref[j] = jnp.dot(m_ref[...], w_ref[...], preferred_element_type=F32)

    @pl.when(j == nj - 1)
    def _():
        ss = jnp.zeros((tm, 1), F32)
        for jj in range(nj):
            t = acc_ref[jj]
            ss = ss + jnp.sum(t * t, axis=-1, keepdims=True)
        inv = lax.rsqrt(ss / (nj * tn) + NORM_EPS)
        for jj in range(nj):
            sl = slice(jj * tn, (jj + 1) * tn)
            gate = _row_select(i, tm, seq, gate_ref[0:1, sl], gate_ref[1:2, sl])
            o_ref[:, sl] = z_ref[:, sl] + gate * (acc_ref[jj] * inv * g_ref[:, sl])


def _outproj(merged, w_out, z, g_post, gate, seq):
    rows, d = merged.shape
    tm = _pick(rows, (384, 256, 128))
    tn = _pick(d, (512, 256))
    nj = d // tn
    return pl.pallas_call(
        functools.partial(_outproj_kernel, tm=tm, tn=tn, nj=nj, seq=seq),
        grid=(rows // tm, nj),
        in_specs=[
            pl.BlockSpec((tm, d), lambda i, j: (i, 0)),
            pl.BlockSpec((d, tn), lambda i, j: (0, j)),
            pl.BlockSpec((tm, d), lambda i, j: (i, 0)),
            pl.BlockSpec((1, d), lambda i, j: (0, 0)),
            pl.BlockSpec((SUBLANES, d), lambda i, j: (0, 0)),
        ],
        out_specs=pl.BlockSpec((tm, d), lambda i, j: (i, 0)),
        out_shape=jax.ShapeDtypeStruct((rows, d), F32),
        scratch_shapes=[pltpu.VMEM((nj, tm, tn), F32)],
        compiler_params=_cparams("parallel", "arbitrary"),
        name="outproj",
    )(merged, w_out, z, g_post, gate)


def _rot_cols(w):
    q = ROPE_DIM // 4
    return jnp.concatenate([-w[..., q:2 * q], w[..., 0:q], -w[..., 3 * q:4 * q], w[..., 2 * q:3 * q]], axis=-1)


def _rope_tables(seq, n_ctx):
    rows = seq // GRID_W
    row = jnp.repeat(jnp.arange(rows, dtype=F32), GRID_W)
    col = jnp.tile(jnp.arange(GRID_W, dtype=F32), rows)
    half = ROPE_DIM // 2
    inv = 1.0 / (ROPE_BASE ** (jnp.arange(0, half, 2, dtype=F32) / half))
    a0 = row[:, None] * inv
    a1 = col[:, None] * inv
    ang = jnp.concatenate([a0, a0, a1, a1], axis=1)
    cos = jnp.concatenate([jnp.cos(ang), jnp.ones((n_ctx, ROPE_DIM), F32)], axis=0)
    sin = jnp.concatenate([jnp.sin(ang), jnp.zeros((n_ctx, ROPE_DIM), F32)], axis=0)
    pad = jnp.zeros((seq + n_ctx, LANES - ROPE_DIM), F32)
    return jnp.concatenate([cos, pad], axis=1), jnp.concatenate([sin, pad], axis=1)


def kernel(x, c, ctx, c_ctx, w_mod, b_mod, g_pre, g_post, w_in, q_norm, kv_norm, w_uq, w_ukv, lam_re, lam_im, log_dt, b_re, b_im, c_re, c_im, d_skip, w_glu, b_glu, w_dw, b_dw, ln_g, ln_b, w_pw, b_pw, w_bm, w_bs, w_bc, w_out):
    batch, seq, d = x.shape
    assert batch == 1, "kernel handles the problem's batch of one"
    n_ctx = ctx.shape[1]
    m = seq + n_ctx
    depth = w_in.shape[0]
    q_rank = q_norm.shape[1]
    kv_rank = kv_norm.shape[1]
    ssm_w = d_skip.shape[1]
    conv_w = b_dw.shape[1]
    heads = w_uq.shape[2] // QK_DIM
    mla_w = heads * V_DIM
    assert ssm_w == conv_w == mla_w and ssm_w % SLAB_W == 0 and seq % n_ctx == 0

    first_col = kv_rank + ROPE_DIM
    head_w = -(-first_col // LANES) * LANES
    rest_w = mla_w + ssm_w + 3 * conv_w + 3 * d
    tn = _pick(math.gcd(math.gcd(first_col - first_col % LANES, ssm_w), math.gcd(q_rank, rest_w)), (512, 256, 128))
    su_off = 0
    mg_off = ssm_w
    sg_off = mg_off + mla_w
    ca_off = sg_off + ssm_w
    cb_off = ca_off + conv_w
    cg_off = cb_off + conv_w
    gm_off = cg_off + conv_w
    gs_off = gm_off + d
    gc_off = gs_off + d
    q_off = gc_off + d

    z = jnp.concatenate([x[0], ctx[0]], axis=0)
    cvec = jnp.concatenate([c, c_ctx[None, :], jnp.zeros((SUBLANES - 1 - batch, d), F32)], axis=0)
    cos, sin = _rope_tables(seq, n_ctx)
    b_mod3 = b_mod[:, None, :]
    tq = _pick(seq, (512, 256))
    tk = _pick(seq, (4096, 2048, 1024, 512, 256))

    for l in range(depth):
        last = l == depth - 1
        rows = seq if last else m
        mod = _modulation(cvec, w_mod, b_mod3, l)
        shift, scale, gate = mod[:, 0:d], mod[:, d:2 * d], mod[:, 2 * d:3 * d]
        h = _prenorm(z, g_pre[l][None, :], shift, scale, seq)
        p_head = _inproj_head(h, w_in, l, head_w)
        p = _inproj_body(h, w_in, l, first_col, ssm_w // tn, q_rank // tn, rest_w // tn, tn)

        wq = w_uq[l].reshape(q_rank, heads, QK_DIM)
        wq = jnp.concatenate([wq, _rot_cols(wq[..., NOPE_DIM:])], axis=-1).transpose(1, 0, 2).astype(BF16)
        wkv = w_ukv[l].reshape(kv_rank, heads, NOPE_DIM + V_DIM)
        wk = wkv[..., :NOPE_DIM].transpose(1, 0, 2).astype(BF16)
        wvt = wkv[..., NOPE_DIM:].transpose(1, 2, 0).astype(BF16)
        q = _qup(p, q_norm[l][None, :], wq, cos, sin, q_off)
        k, vt = _kvup(p_head, kv_norm[l][None, :], wk, wvt, cos, sin)
        o_mla = _attention(q, k, vt, p, mg_off, (0, seq), (0, m), tq, tk)
        if not last:
            o_ctx = _attention(q, k, vt, p, mg_off, (seq, n_ctx), (seq, n_ctx), n_ctx, n_ctx)
            o_mla = jnp.concatenate([o_mla, o_ctx], axis=0)

        ys = []
        for direction in range(2):
            bdb, coef, bdc = _s5_params(lam_re[l, direction], lam_im[l, direction], log_dt[l, direction],
                                        b_re[l, direction], b_im[l, direction], c_re[l, direction],
                                        c_im[l, direction], reverse=direction == 1)
            ys.append(_s5_scan(p, bdb, coef, bdc, seq, su_off, ssm_w, reverse=direction == 1))
        o_ssm = _s5_glu(p, ys[0], ys[1], d_skip[l][None, :], w_glu[l].astype(BF16), b_glu[l][None, :],
                        su_off, sg_off, rows)

        o_conv = _conv_module(p, w_dw[l], b_dw[l][None, :], ln_g[l][None, :], ln_b[l][None, :],
                              w_pw[l].astype(BF16), b_pw[l][None, :], seq, ca_off, cb_off, cg_off, rows)

        merged = _merge(o_mla, o_ssm, o_conv, w_bm[l].astype(BF16), w_bs[l].astype(BF16), w_bc[l].astype(BF16),
                        p, gm_off, gs_off, gc_off)
        z = _outproj(merged, w_out[l].astype(BF16), z, g_post[l][None, :], gate, seq)

    return z[None]
```

```python
import functools
import math

import jax
import jax.numpy as jnp
from jax import lax
from jax.experimental import pallas as pl
from jax.experimental.pallas import tpu as pltpu

NOPE_DIM = 128
ROPE_DIM = 64
V_DIM = 128
QK_DIM = NOPE_DIM + ROPE_DIM
ROPE_BASE = 10000.0
GRID_W = 64
ATTN_SCALE = 1.0 / math.sqrt(NOPE_DIM + ROPE_DIM)
LOG2E = math.log2(math.e)
GROUP_CH = 16
STATE_N = 64
CONV_K = 31
CONV_HALO = 16
NORM_EPS = 1e-6

LANES = 128
SUBLANES = 8
S5_T = 16
S5_GROUPS = LANES // GROUP_CH
S5_STATES = S5_GROUPS * STATE_N
VMEM_LIMIT = 56 * 1024 * 1024

F32 = jnp.float32
BF16 = jnp.bfloat16
NT_DIMS = (((1,), (1,)), ((), ()))


def _pick(n, cands):
    for c in cands:
        if n % c == 0:
            return c
    raise ValueError(f"no tile for {n} in {cands}")


def _cparams(*sem):
    return pltpu.CompilerParams(dimension_semantics=sem, vmem_limit_bytes=VMEM_LIMIT)


def _silu(v):
    return v * jax.nn.sigmoid(v)


def _rms(v, g):
    return v * lax.rsqrt(jnp.mean(v * v, axis=-1, keepdims=True) + NORM_EPS) * g


def _row_select(i, tm, seq, x_row, ctx_row):
    rows = i * tm + lax.broadcasted_iota(jnp.int32, (tm, 1), 0)
    return jnp.where(rows < seq, x_row, ctx_row)


def _mod_kernel(c_ref, w_ref, b_ref, o_ref):
    a = _silu(c_ref[...])
    o_ref[...] = jnp.dot(a.astype(BF16), w_ref[...].astype(BF16), preferred_element_type=F32) + b_ref[...]


def _modulation(cvec, w_mod, b_mod3, layer):
    d = cvec.shape[1]
    n = w_mod.shape[2]
    tn = _pick(n, (512, 256, 128))
    return pl.pallas_call(
        _mod_kernel,
        grid=(n // tn,),
        in_specs=[
            pl.BlockSpec((SUBLANES, d), lambda j: (0, 0)),
            pl.BlockSpec((None, d, tn), lambda j: (layer, 0, j)),
            pl.BlockSpec((None, 1, tn), lambda j: (layer, 0, j)),
        ],
        out_specs=pl.BlockSpec((SUBLANES, tn), lambda j: (0, j)),
        out_shape=jax.ShapeDtypeStruct((SUBLANES, n), F32),
        compiler_params=_cparams("parallel"),
        name="modulation",
    )(cvec, w_mod, b_mod3)


def _prenorm_kernel(z_ref, g_ref, sh_ref, sc_ref, o_ref, *, tm, seq):
    i = pl.program_id(0)
    y = _rms(z_ref[...], g_ref[...])
    shift = _row_select(i, tm, seq, sh_ref[0:1, :], sh_ref[1:2, :])
    scale = _row_select(i, tm, seq, sc_ref[0:1, :], sc_ref[1:2, :])
    o_ref[...] = (y * (1.0 + scale) + shift).astype(BF16)


def _prenorm(z, g, shift, scale, seq):
    m, d = z.shape
    tm = _pick(m, (256, 128))
    vec = pl.BlockSpec((SUBLANES, d), lambda i: (0, 0))
    return pl.pallas_call(
        functools.partial(_prenorm_kernel, tm=tm, seq=seq),
        grid=(m // tm,),
        in_specs=[pl.BlockSpec((tm, d), lambda i: (i, 0)), pl.BlockSpec((1, d), lambda i: (0, 0)), vec, vec],
        out_specs=pl.BlockSpec((tm, d), lambda i: (i, 0)),
        out_shape=jax.ShapeDtypeStruct((m, d), BF16),
        compiler_params=_cparams("parallel"),
        name="prenorm",
    )(z, g, shift, scale)


def _inproj_head_kernel(h_ref, w_ref, o_ref, wb_ref):
    @pl.when(pl.program_id(0) == 0)
    def _():
        wb_ref[...] = w_ref[...].astype(BF16)

    o_ref[...] = lax.dot_general(h_ref[...], wb_ref[...], NT_DIMS, preferred_element_type=F32).astype(BF16)


def _inproj_head(h, w_in_t, layer, head_w):
    m, k = h.shape
    tm = _pick(m, (768, 640, 512, 256))
    return pl.pallas_call(
        _inproj_head_kernel,
        grid=(m // tm,),
        in_specs=[pl.BlockSpec((tm, k), lambda i: (i, 0)), pl.BlockSpec((None, head_w, k), lambda i: (layer, 0, 0))],
        out_specs=pl.BlockSpec((tm, head_w), lambda i: (i, 0)),
        out_shape=jax.ShapeDtypeStruct((m, head_w), BF16),
        scratch_shapes=[pltpu.VMEM((head_w, k), BF16)],
        compiler_params=_cparams("arbitrary"),
        name="inproj_head",
    )(h, w_in_t)


def _inproj_body_kernel(h_ref, wa_ref, wn_ref, o_ref, wb_ref, *, shift, tn):
    @pl.when(pl.program_id(1) == 0)
    def _():
        wb_ref[0:tn - shift, :] = wa_ref[shift:tn, :].astype(BF16)
        wb_ref[tn - shift:tn, :] = wn_ref[...].astype(BF16)

    o_ref[...] = lax.dot_general(h_ref[...], wb_ref[...], NT_DIMS, preferred_element_type=F32).astype(BF16)


def _inproj_body(h, w_in_t, layer, first_col, n_su, n_q, n_rest, tn, shift):
    m, k = h.shape
    tm = _pick(m, (768, 640, 512, 256))
    base = (first_col - shift) // tn
    nj = n_su + n_q + n_rest

    def src(j):
        return jnp.where(j < n_su, j, jnp.where(j < n_su + n_rest, j + n_q, j - n_rest))

    return pl.pallas_call(
        functools.partial(_inproj_body_kernel, shift=shift, tn=tn),
        grid=(nj, m // tm),
        in_specs=[
            pl.BlockSpec((tm, k), lambda j, i: (i, 0)),
            pl.BlockSpec((None, tn, k), lambda j, i: (layer, base + src(j), 0)),
            pl.BlockSpec((None, shift, k), lambda j, i: (layer, (base + src(j) + 1) * (tn // shift), 0)),
        ],
        out_specs=pl.BlockSpec((tm, tn), lambda j, i: (i, j)),
        out_shape=jax.ShapeDtypeStruct((m, nj * tn), BF16),
        scratch_shapes=[pltpu.VMEM((tn, k), BF16)],
        compiler_params=_cparams("parallel", "arbitrary"),
        name="inproj_body",
    )(h, w_in_t, w_in_t)


def _qup_kernel(p_ref, g_ref, w_ref, cos_ref, sin_ref, o_ref, qn_ref):
    @pl.when(pl.program_id(1) == 0)
    def _():
        qn_ref[...] = _rms(p_ref[...].astype(F32), g_ref[...]).astype(BF16)

    y = jnp.dot(qn_ref[...], w_ref[...], preferred_element_type=F32)
    y2 = y[:, NOPE_DIM:]
    r = y2 * cos_ref[...] + pltpu.roll(y2, ROPE_DIM, 1) * sin_ref[...]
    scale = ATTN_SCALE * LOG2E
    o_ref[:, 0:NOPE_DIM] = (y[:, 0:NOPE_DIM] * scale).astype(BF16)
    o_ref[:, NOPE_DIM:QK_DIM] = (r[:, 0:ROPE_DIM] * scale).astype(BF16)


def _qup(p, g, wq, cos, sin, q_off):
    m = p.shape[0]
    heads, q_rank, _ = wq.shape
    tm = _pick(m, (768, 640, 512, 256))
    return pl.pallas_call(
        _qup_kernel,
        grid=(m // tm, heads),
        in_specs=[
            pl.BlockSpec((tm, q_rank), lambda i, h: (i, q_off // q_rank)),
            pl.BlockSpec((1, q_rank), lambda i, h: (0, 0)),
            pl.BlockSpec((None, q_rank, 2 * LANES), lambda i, h: (h, 0, 0)),
            pl.BlockSpec((tm, LANES), lambda i, h: (i, 0)),
            pl.BlockSpec((tm, LANES), lambda i, h: (i, 0)),
        ],
        out_specs=pl.BlockSpec((None, tm, QK_DIM), lambda i, h: (h, i, 0)),
        out_shape=jax.ShapeDtypeStruct((heads, m, QK_DIM), BF16),
        scratch_shapes=[pltpu.VMEM((tm, q_rank), BF16)],
        compiler_params=_cparams("parallel", "arbitrary"),
        name="q_up",
    )(p, g, wq, cos, sin)


def _kvup_kernel(p_ref, kr_ref, g_ref, wk_ref, wvt_ref, cos_ref, sin_ref, k_ref, vt_ref, kvn_ref, krr_ref):
    @pl.when(pl.program_id(1) == 0)
    def _():
        kvn_ref[...] = _rms(p_ref[...].astype(F32), g_ref[...]).astype(BF16)
        kr = kr_ref[...].astype(F32)
        quarter = ROPE_DIM // 4
        lane = lax.broadcasted_iota(jnp.int32, kr.shape, 1)
        rot = jnp.where(lane % (2 * quarter) < quarter,
                        -pltpu.roll(kr, LANES - quarter, 1), pltpu.roll(kr, quarter, 1))
        krr_ref[...] = (kr * cos_ref[...] + rot * sin_ref[...]).astype(BF16)

    kvn = kvn_ref[...]
    k_ref[:, 0:NOPE_DIM] = jnp.dot(kvn, wk_ref[...], preferred_element_type=F32).astype(BF16)
    k_ref[:, NOPE_DIM:QK_DIM] = krr_ref[:, 0:ROPE_DIM]
    vt_ref[...] = lax.dot_general(wvt_ref[...], kvn, NT_DIMS, preferred_element_type=F32).astype(BF16)


def _kvup(p, g, wk, wvt, cos, sin):
    m = p.shape[0]
    heads, kv_rank, _ = wk.shape
    tm = _pick(m, (768, 640, 512, 256))
    return pl.pallas_call(
        _kvup_kernel,
        grid=(m // tm, heads),
        in_specs=[
            pl.BlockSpec((tm, kv_rank), lambda i, h: (i, 0)),
            pl.BlockSpec((tm, LANES), lambda i, h: (i, kv_rank // LANES)),
            pl.BlockSpec((1, kv_rank), lambda i, h: (0, 0)),
            pl.BlockSpec((None, kv_rank, NOPE_DIM), lambda i, h: (h, 0, 0)),
            pl.BlockSpec((None, V_DIM, kv_rank), lambda i, h: (h, 0, 0)),
            pl.BlockSpec((tm, LANES), lambda i, h: (i, 0)),
            pl.BlockSpec((tm, LANES), lambda i, h: (i, 0)),
        ],
        out_specs=[
            pl.BlockSpec((None, tm, QK_DIM), lambda i, h: (h, i, 0)),
            pl.BlockSpec((None, V_DIM, tm), lambda i, h: (h, 0, i)),
        ],
        out_shape=[
            jax.ShapeDtypeStruct((heads, m, QK_DIM), BF16),
            jax.ShapeDtypeStruct((heads, V_DIM, m), BF16),
        ],
        scratch_shapes=[pltpu.VMEM((tm, kv_rank), BF16), pltpu.VMEM((tm, LANES), BF16)],
        compiler_params=_cparams("parallel", "arbitrary"),
        name="kv_up",
    )(p, p, g, wk, wvt, cos, sin)


def _attn_kernel(q_ref, k_ref, vt_ref, mg_ref, o_ref, *, tq, chunks):
    q = q_ref[...]
    m_run = jnp.full((1, tq), -1e30, F32)
    l_run = jnp.zeros((1, tq), F32)
    acc = jnp.zeros((V_DIM, tq), F32)
    for off, size in chunks:
        s = lax.dot_general(k_ref[off:off + size, :], q, NT_DIMS, preferred_element_type=F32)
        m_new = jnp.maximum(m_run, jnp.max(s, axis=0, keepdims=True))
        alpha = jnp.exp2(m_run - m_new)
        p = jnp.exp2(s - m_new)
        l_run = alpha * l_run + jnp.sum(p, axis=0, keepdims=True)
        acc = alpha * acc + jnp.dot(vt_ref[:, off:off + size], p.astype(BF16), preferred_element_type=F32)
        m_run = m_new
    o_ref[...] = ((acc / l_run).T * _silu(mg_ref[...].astype(F32))).astype(BF16)


def _attention(q, k, vt, p, mg_off, q_rows, key_rows, tq, tk):
    heads = q.shape[0]
    q0, nq = q_rows
    k0, nk = key_rows
    assert q0 % tq == 0 and nq % tq == 0 and k0 % nk == 0
    mg_blk = mg_off // V_DIM
    tail = nk % tk
    chunks = ((nk - tail, tail),) if tail else ()
    chunks += tuple((off, tk) for off in range(0, nk - tail, tk))
    return pl.pallas_call(
        functools.partial(_attn_kernel, tq=tq, chunks=chunks),
        grid=(heads, nq // tq),
        in_specs=[
            pl.BlockSpec((None, tq, QK_DIM), lambda h, i: (h, q0 // tq + i, 0)),
            pl.BlockSpec((None, nk, QK_DIM), lambda h, i: (h, k0 // nk, 0)),
            pl.BlockSpec((None, V_DIM, nk), lambda h, i: (h, 0, k0 // nk)),
            pl.BlockSpec((tq, V_DIM), lambda h, i: (q0 // tq + i, mg_blk + h)),
        ],
        out_specs=pl.BlockSpec((tq, V_DIM), lambda h, i: (i, h)),
        out_shape=jax.ShapeDtypeStruct((nq, heads * V_DIM), BF16),
        compiler_params=_cparams("parallel", "parallel"),
        name="attention",
    )(q, k, vt, p)


def _s5_kernel(u_ref, kb_ref, pb_ref, qc_ref, coef_ref, y_ref,
               a_ref, t_ref, pbm_ref, qcm_ref, hs_ref, yc_ref, *, n_chunks, nx_t, nc_t):
    t_len = S5_T
    y_ref[...] = u_ref[...].astype(F32)
    for t in range(t_len):
        a_ref[:, t * LANES:(t + 1) * LANES] = y_ref[pl.ds(t, n_chunks, stride=t_len), :].astype(BF16)
    for t in range(t_len):
        for t2 in range(t_len):
            t_ref[t * LANES:(t + 1) * LANES, t2 * LANES:(t2 + 1) * LANES] = kb_ref[t2 - t + t_len - 1]
    yc_ref[...] = jnp.dot(a_ref[...], t_ref[...], preferred_element_type=F32)

    n_tiles = n_chunks // SUBLANES
    ns = S5_STATES
    row = lax.broadcasted_iota(jnp.int32, (SUBLANES, ns), 0)
    pair_blocks = ns // LANES

    for dr in range(2):
        reverse = dr == 1
        pbm_ref[...] = jnp.zeros_like(pbm_ref)
        qcm_ref[...] = jnp.zeros_like(qcm_ref)
        for part in range(2):
            for t in range(t_len):
                for g in range(S5_GROUPS):
                    r0 = t * LANES + g * GROUP_CH
                    c0 = (part * pair_blocks + g // 2) * LANES
                    pbm_ref[r0:r0 + GROUP_CH, c0:c0 + LANES] = pb_ref[dr, part, t, g]
                    qcm_ref[r0:r0 + GROUP_CH, c0:c0 + LANES] = qc_ref[dr, part, t, g]
        hs_ref[...] = jnp.dot(a_ref[...], pbm_ref[...], preferred_element_type=F32)

        def tile(i, carry, dr=dr, reverse=reverse):
            c_re, c_im = carry
            if reverse:
                idx = n_tiles - 1 - i
            else:
                idx = jnp.where(i < nc_t, nx_t + i, i - nc_t)
            r0 = pl.multiple_of(idx * SUBLANES, SUBLANES)
            re = hs_ref[pl.ds(r0, SUBLANES), 0:ns]
            im = hs_ref[pl.ds(r0, SUBLANES), ns:2 * ns]
            for s, d in enumerate((1, 2, 4)):
                sh = (SUBLANES - d) if reverse else d
                s_re = pltpu.roll(re, sh, 0)
                s_im = pltpu.roll(im, sh, 0)
                a = coef_ref[dr, 2 * s]
                b = coef_ref[dr, 2 * s + 1]
                re, im = re + (a * s_re - b * s_im), im + (a * s_im + b * s_re)
            p_re = coef_ref[dr, 6]
            p_im = coef_ref[dr, 7]
            re, im = re + (p_re * c_re - p_im * c_im), im + (p_re * c_im + p_im * c_re)
            if reverse:
                edge, sh, last = SUBLANES - 1, SUBLANES - 1, 0
            else:
                edge, sh, last = 0, 1, SUBLANES - 1
            hs_ref[pl.ds(r0, SUBLANES), 0:ns] = jnp.where(row == edge, c_re, pltpu.roll(re, sh, 0))
            hs_ref[pl.ds(r0, SUBLANES), ns:2 * ns] = jnp.where(row == edge, c_im, pltpu.roll(im, sh, 0))
            return re[last:last + 1, :], im[last:last + 1, :]

        zero = jnp.zeros((1, ns), F32)
        lax.fori_loop(0, n_tiles, tile, (zero, zero))
        yc_ref[...] += lax.dot_general(hs_ref[...].astype(BF16), qcm_ref[...], NT_DIMS, preferred_element_type=F32)

    for t in range(t_len):
        y_ref[pl.ds(t, n_chunks, stride=t_len), :] = yc_ref[:, t * LANES:(t + 1) * LANES]


def _s5_mix(p, kb, pb, qc, coef, seq, su_off):
    m = p.shape[0]
    slabs = kb.shape[0]
    n_chunks = m // S5_T
    nx_t = seq // (S5_T * SUBLANES)
    nc_t = (m - seq) // (S5_T * SUBLANES)
    row_w = S5_T * LANES
    return pl.pallas_call(
        functools.partial(_s5_kernel, n_chunks=n_chunks, nx_t=nx_t, nc_t=nc_t),
        grid=(slabs,),
        in_specs=[
            pl.BlockSpec((m, LANES), lambda s: (0, su_off // LANES + s)),
            pl.BlockSpec((None, 2 * S5_T - 1, LANES, LANES), lambda s: (s, 0, 0, 0)),
            pl.BlockSpec((None, 2, 2, S5_T, S5_GROUPS, GROUP_CH, LANES), lambda s: (s, 0, 0, 0, 0, 0, 0)),
            pl.BlockSpec((None, 2, 2, S5_T, S5_GROUPS, GROUP_CH, LANES), lambda s: (s, 0, 0, 0, 0, 0, 0)),
            pl.BlockSpec((None, 2, 8, SUBLANES, S5_STATES), lambda s: (s, 0, 0, 0, 0)),
        ],
        out_specs=pl.BlockSpec((m, LANES), lambda s: (0, s)),
        out_shape=jax.ShapeDtypeStruct((m, slabs * LANES), F32),
        scratch_shapes=[
            pltpu.VMEM((n_chunks, row_w), BF16),
            pltpu.VMEM((row_w, row_w), BF16),
            pltpu.VMEM((row_w, 2 * S5_STATES), BF16),
            pltpu.VMEM((row_w, 2 * S5_STATES), BF16),
            pltpu.VMEM((n_chunks, 2 * S5_STATES), F32),
            pltpu.VMEM((n_chunks, row_w), F32),
        ],
        compiler_params=_cparams("parallel"),
        name="s5_mix",
    )(p, kb, pb, qc, coef)


def _s5_params(lam_re, lam_im, log_dt, b_re, b_im, c_re, c_im):
    g = lam_re.shape[1]
    slabs = g // S5_GROUPS
    t_len = S5_T
    lam = lax.complex(lam_re.astype(F32), lam_im.astype(F32))
    lam_dt = lam * jnp.exp(log_dt.astype(F32))[..., None]
    lam_bar = jnp.exp(lam_dt)
    b_bar = ((lam_bar - 1.0) / lam)[..., None] * lax.complex(b_re.astype(F32), b_im.astype(F32))
    c_mat = lax.complex(c_re.astype(F32), c_im.astype(F32))

    def power(k):
        return jnp.exp(lam_dt * k.astype(F32).reshape(k.shape + (1, 1, 1)))

    steps = jnp.arange(t_len)
    kern = jnp.real(jnp.einsum("dgon,kdgn,dgni->kdgoi", c_mat, power(steps), b_bar))
    eye = jnp.eye(S5_GROUPS, dtype=F32)

    def blockdiag(k):
        k = k.reshape(t_len, slabs, S5_GROUPS, GROUP_CH, GROUP_CH)
        return jnp.einsum("ksgoi,gh->ksgiho", k, eye).reshape(t_len, slabs, LANES, LANES)

    kf, kbw = blockdiag(kern[:, 0]), blockdiag(kern[:, 1])
    kb = jnp.concatenate([kbw[:0:-1], (kf[0] + kbw[0])[None], kf[1:]], axis=0)
    kb = kb.transpose(1, 0, 2, 3).astype(BF16)

    odd = (jnp.arange(g) % 2 == 1)[:, None, None]

    def tiles(w):
        z = jnp.zeros_like(w)
        w = jnp.where(odd, jnp.concatenate([z, w], axis=-1), jnp.concatenate([w, z], axis=-1))
        w = w.reshape(t_len, 2, slabs, S5_GROUPS, GROUP_CH, LANES)
        return w.transpose(2, 1, 0, 3, 4, 5)

    def parts(re, im):
        return jnp.stack([tiles(re), tiles(im)], axis=2).astype(BF16)

    pw_in = jnp.stack([power(t_len - 1 - steps)[:, 0], power(steps)[:, 1]], axis=1)
    pz = jnp.einsum("tdgn,dgni->tdgin", pw_in, b_bar)
    pb = parts(jnp.real(pz), jnp.imag(pz))
    pw_out = jnp.stack([power(steps + 1)[:, 0], power(t_len - steps)[:, 1]], axis=1)
    qz = jnp.einsum("dgon,tdgn->tdgon", c_mat, pw_out)
    qc = parts(jnp.real(qz), -jnp.imag(qz))

    rows = jnp.arange(SUBLANES)

    def pair(zc, keep):
        zc = zc * keep[:, :, None, None]

        def lay(w):
            return w.reshape(SUBLANES, 2, slabs, S5_STATES).transpose(2, 1, 0, 3)

        return [lay(jnp.real(zc)), lay(jnp.imag(zc))]

    coefs = []
    for d in (1, 2, 4):
        keep = jnp.stack([rows >= d, rows < SUBLANES - d], axis=1).astype(F32)
        coefs += pair(power(jnp.full((SUBLANES,), d * t_len)), keep)
    lead = jnp.stack([power((rows + 1) * t_len)[:, 0], power((SUBLANES - rows) * t_len)[:, 1]], axis=1)
    coefs += pair(lead, jnp.ones((SUBLANES, 2), F32))
    coef = jnp.stack(coefs, axis=2).astype(F32)
    return kb, pb, qc, coef


def _glu_kernel(u_ref, y_ref, d_ref, w_ref, b_ref, sg_ref, o_ref):
    y = d_ref[...] * u_ref[...].astype(F32) + y_ref[...]
    z = jax.nn.gelu(y)
    g = jnp.dot(z.astype(BF16), w_ref[...], preferred_element_type=F32) + b_ref[...]
    o_ref[...] = (z * jax.nn.sigmoid(g) * _silu(sg_ref[...].astype(F32))).astype(BF16)


def _s5_glu(p, y, d_skip, w_glu, b_glu, su_off, sg_off, rows):
    w = y.shape[1]
    tm = _pick(rows, (256, 128))
    row = pl.BlockSpec((tm, w), lambda i: (i, 0))
    vec = pl.BlockSpec((1, w), lambda i: (0, 0))
    return pl.pallas_call(
        _glu_kernel,
        grid=(rows // tm,),
        in_specs=[
            pl.BlockSpec((tm, w), lambda i: (i, su_off // w)), row, vec,
            pl.BlockSpec((w, w), lambda i: (0, 0)), vec,
            pl.BlockSpec((tm, w), lambda i: (i, sg_off // w)),
        ],
        out_specs=row,
        out_shape=jax.ShapeDtypeStruct((rows, w), BF16),
        compiler_params=_cparams("parallel"),
        name="s5_glu",
    )(p, y, d_skip, w_glu, b_glu, p)


def _conv_kernel(a_ref, b_ref, ap_ref, bp_ref, an_ref, bn_ref, wdw_ref, bdw_ref, lng_ref, lnb_ref,
                 wpw_ref, bpw_ref, cg_ref, o_ref, u_ref, c_ref, *, tm, seq, n_rows):
    i = pl.program_id(0)
    row0 = i * tm
    w = u_ref.shape[1]
    has_prev = jnp.logical_and(row0 != 0, row0 != seq)
    has_next = jnp.logical_and(row0 + tm != seq, row0 + tm != n_rows)

    def glu(a, b):
        return a[...].astype(F32) * jax.nn.sigmoid(b[...].astype(F32))

    u_ref[0:CONV_HALO, :] = jnp.where(has_prev, glu(ap_ref, bp_ref), 0.0)
    u_ref[CONV_HALO:CONV_HALO + tm, :] = glu(a_ref, b_ref)
    u_ref[CONV_HALO + tm:2 * CONV_HALO + tm, :] = jnp.where(has_next, glu(an_ref, bn_ref), 0.0)

    rc = 32
    lc = _pick(w, (512, 256))
    first = CONV_HALO - CONV_K // 2

    def rows(ci, _):
        r = pl.multiple_of(ci * rc, rc)
        for l in range(0, w, lc):
            win = u_ref[pl.ds(r, rc + 2 * CONV_HALO), l:l + lc]
            acc = jnp.zeros((rc, lc), F32)
            for k in range(CONV_K):
                tap = pltpu.roll(win, rc + 2 * CONV_HALO - (first + k), 0)[0:rc, :]
                acc = acc + wdw_ref[k:k + 1, l:l + lc] * tap
            c_ref[pl.ds(r, rc), l:l + lc] = acc + bdw_ref[:, l:l + lc]
        return 0

    lax.fori_loop(0, tm // rc, rows, 0)

    c = c_ref[...]
    mu = jnp.mean(c, axis=-1, keepdims=True)
    var = jnp.mean(jnp.square(c - mu), axis=-1, keepdims=True)
    y = _silu((c - mu) * lax.rsqrt(var + NORM_EPS) * lng_ref[...] + lnb_ref[...])
    out = jnp.dot(y.astype(BF16), wpw_ref[...], preferred_element_type=F32) + bpw_ref[...]
    o_ref[...] = (out * _silu(cg_ref[...].astype(F32))).astype(BF16)


def _conv_module(p, w_dw, b_dw, ln_g, ln_b, w_pw, b_pw, seq, ca_off, cb_off, cg_off, rows):
    m = p.shape[0]
    w = w_pw.shape[0]
    tm = _pick(math.gcd(seq, m - seq), (256, 128))
    hb = tm // CONV_HALO
    n_halo = m // CONV_HALO
    ca, cb, cg = ca_off // w, cb_off // w, cg_off // w

    def prev(i):
        return jnp.maximum(i * hb - 1, 0)

    def nxt(i):
        return jnp.minimum((i + 1) * hb, n_halo - 1)

    vec = pl.BlockSpec((1, w), lambda i: (0, 0))
    return pl.pallas_call(
        functools.partial(_conv_kernel, tm=tm, seq=seq, n_rows=m),
        grid=(rows // tm,),
        in_specs=[
            pl.BlockSpec((tm, w), lambda i: (i, ca)),
            pl.BlockSpec((tm, w), lambda i: (i, cb)),
            pl.BlockSpec((CONV_HALO, w), lambda i: (prev(i), ca)),
            pl.BlockSpec((CONV_HALO, w), lambda i: (prev(i), cb)),
            pl.BlockSpec((CONV_HALO, w), lambda i: (nxt(i), ca)),
            pl.BlockSpec((CONV_HALO, w), lambda i: (nxt(i), cb)),
            pl.BlockSpec((CONV_K, w), lambda i: (0, 0)),
            vec, vec, vec,
            pl.BlockSpec((w, w), lambda i: (0, 0)),
            vec,
            pl.BlockSpec((tm, w), lambda i: (i, cg)),
        ],
        out_specs=pl.BlockSpec((tm, w), lambda i: (i, 0)),
        out_shape=jax.ShapeDtypeStruct((rows, w), BF16),
        scratch_shapes=[pltpu.VMEM((tm + 2 * CONV_HALO, w), F32), pltpu.VMEM((tm, w), F32)],
        compiler_params=_cparams("parallel"),
        name="conv_module",
    )(p, p, p, p, p, p, w_dw, b_dw, ln_g, ln_b, w_pw, b_pw, p)


def _merge_kernel(a_ref, s_ref, c_ref, wm_ref, ws_ref, wc_ref, gm_ref, gs_ref, gc_ref, o_ref):
    def branch(x_ref, w_ref, g_ref):
        y = jnp.dot(x_ref[...], w_ref[...], preferred_element_type=F32)
        return jax.nn.sigmoid(g_ref[...].astype(F32)) * y

    o_ref[...] = (branch(a_ref, wm_ref, gm_ref) + branch(s_ref, ws_ref, gs_ref)
                  + branch(c_ref, wc_ref, gc_ref)).astype(BF16)


def _merge(o_mla, o_ssm, o_conv, w_bm, w_bs, w_bc, p, gm_off, gs_off, gc_off):
    rows = o_mla.shape[0]
    d = w_bm.shape[1]
    tm = _pick(rows, (768, 640, 512, 256))
    tn = _pick(d, (512, 256))

    def lhs(x):
        return pl.BlockSpec((tm, x.shape[1]), lambda i, j: (i, 0))

    def wgt(x):
        return pl.BlockSpec((x.shape[0], tn), lambda i, j: (0, j))

    def gate(off):
        return pl.BlockSpec((tm, tn), lambda i, j: (i, off // tn + j))

    return pl.pallas_call(
        _merge_kernel,
        grid=(rows // tm, d // tn),
        in_specs=[lhs(o_mla), lhs(o_ssm), lhs(o_conv), wgt(w_bm), wgt(w_bs), wgt(w_bc),
                  gate(gm_off), gate(gs_off), gate(gc_off)],
        out_specs=pl.BlockSpec((tm, tn), lambda i, j: (i, j)),
        out_shape=jax.ShapeDtypeStruct((rows, d), BF16),
        compiler_params=_cparams("parallel", "arbitrary"),
        name="merge",
    )(o_mla, o_ssm, o_conv, w_bm, w_bs, w_bc, p, p, p)


def _outproj_kernel(m_ref, w_ref, z_ref, g_ref, gate_ref, o_ref, acc_ref, *, tm, tn, nj, seq):
    i = pl.program_id(0)
    j = pl.program_id(1)
    acc_ref[j] = jnp.dot(m_ref[...], w_ref[...], preferred_element_type=F32)

    @pl.when(j == nj - 1)
    def _():
        ss = jnp.zeros((tm, 1), F32)
        for jj in range(nj):
            t = acc_ref[jj]
            ss = ss + jnp.sum(t * t, axis=-1, keepdims=True)
        inv = lax.rsqrt(ss / (nj * tn) + NORM_EPS)
        for jj in range(nj):
            sl = slice(jj * tn, (jj + 1) * tn)
            gate = _row_select(i, tm, seq, gate_ref[0:1, sl], gate_ref[1:2, sl])
            o_ref[:, sl] = z_ref[:, sl] + gate * (acc_ref[jj] * inv * g_ref[:, sl])


def _outproj(merged, w_out, z, g_post, gate, seq):
    rows, d = merged.shape
    tm = _pick(rows, (384, 256, 128))
    tn = _pick(d, (512, 256))
    nj = d // tn
    return pl.pallas_call(
        functools.partial(_outproj_kernel, tm=tm, tn=tn, nj=nj, seq=seq),
        grid=(rows // tm, nj),
        in_specs=[
            pl.BlockSpec((tm, d), lambda i, j: (i, 0)),
            pl.BlockSpec((d, tn), lambda i, j: (0, j)),
            pl.BlockSpec((tm, d), lambda i, j: (i, 0)),
            pl.BlockSpec((1, d), lambda i, j: (0, 0)),
            pl.BlockSpec((SUBLANES, d), lambda i, j: (0, 0)),
        ],
        out_specs=pl.BlockSpec((tm, d), lambda i, j: (i, 0)),
        out_shape=jax.ShapeDtypeStruct((rows, d), F32),
        scratch_shapes=[pltpu.VMEM((nj, tm, tn), F32)],
        compiler_params=_cparams("parallel", "arbitrary"),
        name="outproj",
    )(merged, w_out, z, g_post, gate)


def _rot_cols(w):
    q = ROPE_DIM // 4
    return jnp.concatenate([-w[..., q:2 * q], w[..., 0:q], -w[..., 3 * q:4 * q], w[..., 2 * q:3 * q]], axis=-1)


def _rope_tables(seq, n_ctx):
    rows = seq // GRID_W
    row = jnp.repeat(jnp.arange(rows, dtype=F32), GRID_W)
    col = jnp.tile(jnp.arange(GRID_W, dtype=F32), rows)
    half = ROPE_DIM // 2
    inv = 1.0 / (ROPE_BASE ** (jnp.arange(0, half, 2, dtype=F32) / half))
    a0 = row[:, None] * inv
    a1 = col[:, None] * inv
    ang = jnp.concatenate([a0, a0, a1, a1], axis=1)
    cos = jnp.concatenate([jnp.cos(ang), jnp.ones((n_ctx, ROPE_DIM), F32)], axis=0)
    sin = jnp.concatenate([jnp.sin(ang), jnp.zeros((n_ctx, ROPE_DIM), F32)], axis=0)
    pad = jnp.zeros((seq + n_ctx, LANES - ROPE_DIM), F32)
    return jnp.concatenate([cos, pad], axis=1), jnp.concatenate([sin, pad], axis=1)


def kernel(x, c, ctx, c_ctx, w_mod, b_mod, g_pre, g_post, w_in, q_norm, kv_norm, w_uq, w_ukv, lam_re, lam_im, log_dt, b_re, b_im, c_re, c_im, d_skip, w_glu, b_glu, w_dw, b_dw, ln_g, ln_b, w_pw, b_pw, w_bm, w_bs, w_bc, w_out):
    batch, seq, d = x.shape
    assert batch == 1, "kernel handles the problem's batch of one"
    n_ctx = ctx.shape[1]
    m = seq + n_ctx
    depth = w_in.shape[0]
    q_rank = q_norm.shape[1]
    kv_rank = kv_norm.shape[1]
    ssm_w = d_skip.shape[1]
    conv_w = b_dw.shape[1]
    heads = w_uq.shape[2] // QK_DIM
    mla_w = heads * V_DIM
    assert ssm_w == conv_w == mla_w and ssm_w % LANES == 0 and seq % n_ctx == 0
    assert seq % (S5_T * SUBLANES) == 0 and n_ctx % (S5_T * SUBLANES) == 0

    first_col = kv_rank + ROPE_DIM
    head_w = -(-first_col // LANES) * LANES
    rest_w = mla_w + ssm_w + 3 * conv_w + 3 * d
    row_shift = first_col % LANES
    assert row_shift % 16 == 0 and row_shift > 0
    tn = _pick(math.gcd(math.gcd(first_col - row_shift, ssm_w), math.gcd(q_rank, rest_w)), (512, 256, 128))
    w_in_t = jnp.swapaxes(w_in, 1, 2)
    su_off = 0
    mg_off = ssm_w
    sg_off = mg_off + mla_w
    ca_off = sg_off + ssm_w
    cb_off = ca_off + conv_w
    cg_off = cb_off + conv_w
    gm_off = cg_off + conv_w
    gs_off = gm_off + d
    gc_off = gs_off + d
    q_off = gc_off + d

    z = jnp.concatenate([x[0], ctx[0]], axis=0)
    cvec = jnp.concatenate([c, c_ctx[None, :], jnp.zeros((SUBLANES - 1 - batch, d), F32)], axis=0)
    cos, sin = _rope_tables(seq, n_ctx)
    b_mod3 = b_mod[:, None, :]
    tq = _pick(seq, (512, 256))
    tk = _pick(seq, (4096, 2048, 1024, 512, 256))

    for l in range(depth):
        last = l == depth - 1
        rows = seq if last else m
        mod = _modulation(cvec, w_mod, b_mod3, l)
        shift, scale, gate = mod[:, 0:d], mod[:, d:2 * d], mod[:, 2 * d:3 * d]
        h = _prenorm(z, g_pre[l][None, :], shift, scale, seq)
        p_head = _inproj_head(h, w_in_t, l, head_w)
        p = _inproj_body(h, w_in_t, l, first_col, ssm_w // tn, q_rank // tn, rest_w // tn, tn, row_shift)

        wq = w_uq[l].reshape(q_rank, heads, QK_DIM)
        wq = jnp.concatenate([wq, _rot_cols(wq[..., NOPE_DIM:])], axis=-1).transpose(1, 0, 2).astype(BF16)
        wkv = w_ukv[l].reshape(kv_rank, heads, NOPE_DIM + V_DIM)
        wk = wkv[..., :NOPE_DIM].transpose(1, 0, 2).astype(BF16)
        wvt = wkv[..., NOPE_DIM:].transpose(1, 2, 0).astype(BF16)
        q = _qup(p, q_norm[l][None, :], wq, cos, sin, q_off)
        k, vt = _kvup(p_head, kv_norm[l][None, :], wk, wvt, cos, sin)
        o_mla = _attention(q, k, vt, p, mg_off, (0, seq), (0, m), tq, tk)
        if not last:
            o_ctx = _attention(q, k, vt, p, mg_off, (seq, n_ctx), (seq, n_ctx), n_ctx, n_ctx)
            o_mla = jnp.concatenate([o_mla, o_ctx], axis=0)

        kb, pb, qc, coef = _s5_params(lam_re[l], lam_im[l], log_dt[l], b_re[l], b_im[l], c_re[l], c_im[l])
        y_ssm = _s5_mix(p, kb, pb, qc, coef, seq, su_off)
        o_ssm = _s5_glu(p, y_ssm, d_skip[l][None, :], w_glu[l].astype(BF16), b_glu[l][None, :],
                        su_off, sg_off, rows)

        o_conv = _conv_module(p, w_dw[l], b_dw[l][None, :], ln_g[l][None, :], ln_b[l][None, :],
                              w_pw[l].astype(BF16), b_pw[l][None, :], seq, ca_off, cb_off, cg_off, rows)

        merged = _merge(o_mla, o_ssm, o_conv, w_bm[l].astype(BF16), w_bs[l].astype(BF16), w_bc[l].astype(BF16),
                        p, gm_off, gs_off, gc_off)
        z = _outproj(merged, w_out[l].astype(BF16), z, g_post[l][None, :], gate, seq)

    return z[None]
```

```python
import functools
import math

import jax
import jax.numpy as jnp
from jax import lax
from jax.experimental import pallas as pl
from jax.experimental.pallas import tpu as pltpu

NOPE_DIM = 128
ROPE_DIM = 64
V_DIM = 128
QK_DIM = NOPE_DIM + ROPE_DIM
ROPE_BASE = 10000.0
GRID_W = 64
ATTN_SCALE = 1.0 / math.sqrt(NOPE_DIM + ROPE_DIM)
LOG2E = math.log2(math.e)
GROUP_CH = 16
STATE_N = 64
CONV_K = 31
CONV_HALO = 16
NORM_EPS = 1e-6

LANES = 128
SUBLANES = 8
S5_T = 16
S5_GROUPS = LANES // GROUP_CH
S5_STATES = S5_GROUPS * STATE_N
VMEM_LIMIT = 56 * 1024 * 1024

F32 = jnp.float32
BF16 = jnp.bfloat16
NT_DIMS = (((1,), (1,)), ((), ()))


def _pick(n, cands):
    for c in cands:
        if n % c == 0:
            return c
    raise ValueError(f"no tile for {n} in {cands}")


def _cparams(*sem):
    return pltpu.CompilerParams(dimension_semantics=sem, vmem_limit_bytes=VMEM_LIMIT)


def _silu(v):
    return v * jax.nn.sigmoid(v)


def _rms(v, g):
    return v * lax.rsqrt(jnp.mean(v * v, axis=-1, keepdims=True) + NORM_EPS) * g


def _row_select(i, tm, seq, x_row, ctx_row):
    rows = i * tm + lax.broadcasted_iota(jnp.int32, (tm, 1), 0)
    return jnp.where(rows < seq, x_row, ctx_row)


def _mod_kernel(c_ref, w_ref, b_ref, o_ref):
    a = _silu(c_ref[...])
    o_ref[...] = jnp.dot(a.astype(BF16), w_ref[...].astype(BF16), preferred_element_type=F32) + b_ref[...]


def _modulation(cvec, w_mod, b_mod3, layer):
    d = cvec.shape[1]
    n = w_mod.shape[2]
    tn = _pick(n, (512, 256, 128))
    return pl.pallas_call(
        _mod_kernel,
        grid=(n // tn,),
        in_specs=[
            pl.BlockSpec((SUBLANES, d), lambda j: (0, 0)),
            pl.BlockSpec((None, d, tn), lambda j: (layer, 0, j)),
            pl.BlockSpec((None, 1, tn), lambda j: (layer, 0, j)),
        ],
        out_specs=pl.BlockSpec((SUBLANES, tn), lambda j: (0, j)),
        out_shape=jax.ShapeDtypeStruct((SUBLANES, n), F32),
        compiler_params=_cparams("parallel"),
        name="modulation",
    )(cvec, w_mod, b_mod3)


def _prenorm_kernel(z_ref, g_ref, sh_ref, sc_ref, o_ref, *, tm, seq):
    i = pl.program_id(0)
    y = _rms(z_ref[...], g_ref[...])
    shift = _row_select(i, tm, seq, sh_ref[0:1, :], sh_ref[1:2, :])
    scale = _row_select(i, tm, seq, sc_ref[0:1, :], sc_ref[1:2, :])
    o_ref[...] = (y * (1.0 + scale) + shift).astype(BF16)


def _prenorm(z, g, shift, scale, seq):
    m, d = z.shape
    tm = _pick(m, (256, 128))
    vec = pl.BlockSpec((SUBLANES, d), lambda i: (0, 0))
    return pl.pallas_call(
        functools.partial(_prenorm_kernel, tm=tm, seq=seq),
        grid=(m // tm,),
        in_specs=[pl.BlockSpec((tm, d), lambda i: (i, 0)), pl.BlockSpec((1, d), lambda i: (0, 0)), vec, vec],
        out_specs=pl.BlockSpec((tm, d), lambda i: (i, 0)),
        out_shape=jax.ShapeDtypeStruct((m, d), BF16),
        compiler_params=_cparams("parallel"),
        name="prenorm",
    )(z, g, shift, scale)


def _inproj_head_kernel(h_ref, w_ref, o_ref, wb_ref):
    @pl.when(pl.program_id(0) == 0)
    def _():
        wb_ref[...] = w_ref[...].astype(BF16)

    o_ref[...] = lax.dot_general(h_ref[...], wb_ref[...], NT_DIMS, preferred_element_type=F32).astype(BF16)


def _inproj_head(h, w_in_t, layer, head_w):
    m, k = h.shape
    tm = _pick(m, (768, 640, 512, 256))
    return pl.pallas_call(
        _inproj_head_kernel,
        grid=(m // tm,),
        in_specs=[pl.BlockSpec((tm, k), lambda i: (i, 0)), pl.BlockSpec((None, head_w, k), lambda i: (layer, 0, 0))],
        out_specs=pl.BlockSpec((tm, head_w), lambda i: (i, 0)),
        out_shape=jax.ShapeDtypeStruct((m, head_w), BF16),
        scratch_shapes=[pltpu.VMEM((head_w, k), BF16)],
        compiler_params=_cparams("arbitrary"),
        name="inproj_head",
    )(h, w_in_t)


def _inproj_body_kernel(h_ref, wa_ref, wn_ref, o_ref, wb_ref, *, shift, tn):
    @pl.when(pl.program_id(1) == 0)
    def _():
        wb_ref[0:tn - shift, :] = wa_ref[shift:tn, :].astype(BF16)
        wb_ref[tn - shift:tn, :] = wn_ref[...].astype(BF16)

    o_ref[...] = lax.dot_general(h_ref[...], wb_ref[...], NT_DIMS, preferred_element_type=F32).astype(BF16)


def _inproj_body(h, w_in_t, layer, first_col, n_su, n_q, n_rest, tn, shift):
    m, k = h.shape
    tm = _pick(m, (1408, 768, 640, 512, 256))
    base = (first_col - shift) // tn
    nj = n_su + n_q + n_rest

    def src(j):
        return jnp.where(j < n_su, j, jnp.where(j < n_su + n_rest, j + n_q, j - n_rest))

    return pl.pallas_call(
        functools.partial(_inproj_body_kernel, shift=shift, tn=tn),
        grid=(nj, m // tm),
        in_specs=[
            pl.BlockSpec((tm, k), lambda j, i: (i, 0)),
            pl.BlockSpec((None, tn, k), lambda j, i: (layer, base + src(j), 0)),
            pl.BlockSpec((None, shift, k), lambda j, i: (layer, (base + src(j) + 1) * (tn // shift), 0)),
        ],
        out_specs=pl.BlockSpec((tm, tn), lambda j, i: (i, j)),
        out_shape=jax.ShapeDtypeStruct((m, nj * tn), BF16),
        scratch_shapes=[pltpu.VMEM((tn, k), BF16)],
        compiler_params=_cparams("parallel", "arbitrary"),
        name="inproj_body",
    )(h, w_in_t, w_in_t)


def _qup_kernel(p_ref, g_ref, w_ref, cos_ref, sin_ref, o_ref, *, heads):
    qn = _rms(p_ref[...].astype(F32), g_ref[...]).astype(BF16)
    y = jnp.dot(qn, w_ref[...], preferred_element_type=F32)
    cos, sin = cos_ref[...], sin_ref[...]
    scale = ATTN_SCALE * LOG2E
    for h in range(heads):
        yh = y[:, h * 2 * LANES:(h + 1) * 2 * LANES]
        y2 = yh[:, NOPE_DIM:]
        r = y2 * cos + pltpu.roll(y2, ROPE_DIM, 1) * sin
        o_ref[h, :, 0:NOPE_DIM] = (yh[:, 0:NOPE_DIM] * scale).astype(BF16)
        o_ref[h, :, NOPE_DIM:QK_DIM] = (r[:, 0:ROPE_DIM] * scale).astype(BF16)


def _qup(p, g, wq, cos, sin, q_off, heads):
    m = p.shape[0]
    q_rank = wq.shape[0]
    tm = _pick(m, (384, 256, 128))
    return pl.pallas_call(
        functools.partial(_qup_kernel, heads=heads),
        grid=(m // tm,),
        in_specs=[
            pl.BlockSpec((tm, q_rank), lambda i: (i, q_off // q_rank)),
            pl.BlockSpec((1, q_rank), lambda i: (0, 0)),
            pl.BlockSpec(wq.shape, lambda i: (0, 0)),
            pl.BlockSpec((tm, LANES), lambda i: (i, 0)),
            pl.BlockSpec((tm, LANES), lambda i: (i, 0)),
        ],
        out_specs=pl.BlockSpec((heads, tm, QK_DIM), lambda i: (0, i, 0)),
        out_shape=jax.ShapeDtypeStruct((heads, m, QK_DIM), BF16),
        compiler_params=_cparams("parallel"),
        name="q_up",
    )(p, g, wq, cos, sin)


def _kvup_kernel(p_ref, kr_ref, g_ref, wk_ref, wvt_ref, cos_ref, sin_ref, k_ref, vt_ref, *, heads):
    kvn = _rms(p_ref[...].astype(F32), g_ref[...]).astype(BF16)
    kr = kr_ref[...].astype(F32)
    quarter = ROPE_DIM // 4
    lane = lax.broadcasted_iota(jnp.int32, kr.shape, 1)
    rot = jnp.where(lane % (2 * quarter) < quarter, -pltpu.roll(kr, LANES - quarter, 1), pltpu.roll(kr, quarter, 1))
    krr = (kr * cos_ref[...] + rot * sin_ref[...]).astype(BF16)
    yk = jnp.dot(kvn, wk_ref[...], preferred_element_type=F32)
    yv = lax.dot_general(wvt_ref[...], kvn, NT_DIMS, preferred_element_type=F32)
    for h in range(heads):
        k_ref[h, :, 0:NOPE_DIM] = yk[:, h * NOPE_DIM:(h + 1) * NOPE_DIM].astype(BF16)
        k_ref[h, :, NOPE_DIM:QK_DIM] = krr[:, 0:ROPE_DIM]
        vt_ref[h] = yv[h * V_DIM:(h + 1) * V_DIM, :].astype(BF16)


def _kvup(p, g, wk, wvt, cos, sin, heads):
    m = p.shape[0]
    kv_rank = wk.shape[0]
    tm = _pick(m, (384, 256, 128))
    return pl.pallas_call(
        functools.partial(_kvup_kernel, heads=heads),
        grid=(m // tm,),
        in_specs=[
            pl.BlockSpec((tm, kv_rank), lambda i: (i, 0)),
            pl.BlockSpec((tm, LANES), lambda i: (i, kv_rank // LANES)),
            pl.BlockSpec((1, kv_rank), lambda i: (0, 0)),
            pl.BlockSpec(wk.shape, lambda i: (0, 0)),
            pl.BlockSpec(wvt.shape, lambda i: (0, 0)),
            pl.BlockSpec((tm, LANES), lambda i: (i, 0)),
            pl.BlockSpec((tm, LANES), lambda i: (i, 0)),
        ],
        out_specs=[
            pl.BlockSpec((heads, tm, QK_DIM), lambda i: (0, i, 0)),
            pl.BlockSpec((heads, V_DIM, tm), lambda i: (0, 0, i)),
        ],
        out_shape=[
            jax.ShapeDtypeStruct((heads, m, QK_DIM), BF16),
            jax.ShapeDtypeStruct((heads, V_DIM, m), BF16),
        ],
        compiler_params=_cparams("parallel"),
        name="kv_up",
    )(p, p, g, wk, wvt, cos, sin)


def _attn_kernel(q_ref, k_ref, vt_ref, mg_ref, o_ref, *, tq, chunks):
    q = q_ref[...]
    m_run = jnp.full((1, tq), -1e30, F32)
    l_run = jnp.zeros((1, tq), F32)
    acc = jnp.zeros((V_DIM, tq), F32)
    for off, size in chunks:
        s = lax.dot_general(k_ref[off:off + size, :], q, NT_DIMS, preferred_element_type=F32)
        m_new = jnp.maximum(m_run, jnp.max(s, axis=0, keepdims=True))
        alpha = jnp.exp2(m_run - m_new)
        p = jnp.exp2(s - m_new)
        l_run = alpha * l_run + jnp.sum(p, axis=0, keepdims=True)
        acc = alpha * acc + jnp.dot(vt_ref[:, off:off + size], p.astype(BF16), preferred_element_type=F32)
        m_run = m_new
    o_ref[...] = ((acc / l_run).T * _silu(mg_ref[...].astype(F32))).astype(BF16)


def _attention(q, k, vt, p, mg_off, q_rows, key_rows, tq, tk):
    heads = q.shape[0]
    q0, nq = q_rows
    k0, nk = key_rows
    assert q0 % tq == 0 and nq % tq == 0 and k0 % nk == 0
    mg_blk = mg_off // V_DIM
    tail = nk % tk
    chunks = ((nk - tail, tail),) if tail else ()
    chunks += tuple((off, tk) for off in range(0, nk - tail, tk))
    return pl.pallas_call(
        functools.partial(_attn_kernel, tq=tq, chunks=chunks),
        grid=(heads, nq // tq),
        in_specs=[
            pl.BlockSpec((None, tq, QK_DIM), lambda h, i: (h, q0 // tq + i, 0)),
            pl.BlockSpec((None, nk, QK_DIM), lambda h, i: (h, k0 // nk, 0)),
            pl.BlockSpec((None, V_DIM, nk), lambda h, i: (h, 0, k0 // nk)),
            pl.BlockSpec((tq, V_DIM), lambda h, i: (q0 // tq + i, mg_blk + h)),
        ],
        out_specs=pl.BlockSpec((tq, V_DIM), lambda h, i: (i, h)),
        out_shape=jax.ShapeDtypeStruct((nq, heads * V_DIM), BF16),
        compiler_params=_cparams("parallel", "parallel"),
        name="attention",
    )(q, k, vt, p)


def _s5_kernel(u_ref, kb_ref, pb_ref, qc_ref, coef_ref, y_ref,
               a_ref, t_ref, pbm_ref, qcm_ref, hs_ref, yc_ref, *, n_chunks, nx_t, nc_t):
    t_len = S5_T
    y_ref[...] = u_ref[...].astype(F32)
    for t in range(t_len):
        a_ref[:, t * LANES:(t + 1) * LANES] = y_ref[pl.ds(t, n_chunks, stride=t_len), :].astype(BF16)
    for t in range(t_len):
        for t2 in range(t_len):
            t_ref[t * LANES:(t + 1) * LANES, t2 * LANES:(t2 + 1) * LANES] = kb_ref[t2 - t + t_len - 1]
    yc_ref[...] = jnp.dot(a_ref[...], t_ref[...], preferred_element_type=F32)

    n_tiles = n_chunks // SUBLANES
    ns = S5_STATES
    row = lax.broadcasted_iota(jnp.int32, (SUBLANES, ns), 0)
    pair_blocks = ns // LANES

    for dr in range(2):
        reverse = dr == 1
        pbm_ref[...] = jnp.zeros_like(pbm_ref)
        qcm_ref[...] = jnp.zeros_like(qcm_ref)
        for part in range(2):
            for t in range(t_len):
                for g in range(S5_GROUPS):
                    r0 = t * LANES + g * GROUP_CH
                    c0 = (part * pair_blocks + g // 2) * LANES
                    pbm_ref[r0:r0 + GROUP_CH, c0:c0 + LANES] = pb_ref[dr, part, t, g]
                    qcm_ref[r0:r0 + GROUP_CH, c0:c0 + LANES] = qc_ref[dr, part, t, g]
        hs_ref[...] = jnp.dot(a_ref[...], pbm_ref[...], preferred_element_type=F32)

        def tile(i, carry, dr=dr, reverse=reverse):
            c_re, c_im = carry
            if reverse:
                idx = n_tiles - 1 - i
            else:
                idx = jnp.where(i < nc_t, nx_t + i, i - nc_t)
            r0 = pl.multiple_of(idx * SUBLANES, SUBLANES)
            re = hs_ref[pl.ds(r0, SUBLANES), 0:ns]
            im = hs_ref[pl.ds(r0, SUBLANES), ns:2 * ns]
            for s, d in enumerate((1, 2, 4)):
                sh = (SUBLANES - d) if reverse else d
                s_re = pltpu.roll(re, sh, 0)
                s_im = pltpu.roll(im, sh, 0)
                a = coef_ref[dr, 2 * s]
                b = coef_ref[dr, 2 * s + 1]
                re, im = re + (a * s_re - b * s_im), im + (a * s_im + b * s_re)
            p_re = coef_ref[dr, 6]
            p_im = coef_ref[dr, 7]
            re, im = re + (p_re * c_re - p_im * c_im), im + (p_re * c_im + p_im * c_re)
            if reverse:
                edge, sh, last = SUBLANES - 1, SUBLANES - 1, 0
            else:
                edge, sh, last = 0, 1, SUBLANES - 1
            hs_ref[pl.ds(r0, SUBLANES), 0:ns] = jnp.where(row == edge, c_re, pltpu.roll(re, sh, 0))
            hs_ref[pl.ds(r0, SUBLANES), ns:2 * ns] = jnp.where(row == edge, c_im, pltpu.roll(im, sh, 0))
            return re[last:last + 1, :], im[last:last + 1, :]

        zero = jnp.zeros((1, ns), F32)
        lax.fori_loop(0, n_tiles, tile, (zero, zero))
        yc_ref[...] += lax.dot_general(hs_ref[...].astype(BF16), qcm_ref[...], NT_DIMS, preferred_element_type=F32)

    for t in range(t_len):
        y_ref[pl.ds(t, n_chunks, stride=t_len), :] = yc_ref[:, t * LANES:(t + 1) * LANES]


def _s5_mix(p, kb, pb, qc, coef, layer, seq, su_off):
    m = p.shape[0]
    slabs = kb.shape[1]
    n_chunks = m // S5_T
    nx_t = seq // (S5_T * SUBLANES)
    nc_t = (m - seq) // (S5_T * SUBLANES)
    row_w = S5_T * LANES

    def per_slab(*dims):
        return pl.BlockSpec((None, None) + dims, lambda s: (layer, s) + (0,) * len(dims))

    return pl.pallas_call(
        functools.partial(_s5_kernel, n_chunks=n_chunks, nx_t=nx_t, nc_t=nc_t),
        grid=(slabs,),
        in_specs=[
            pl.BlockSpec((m, LANES), lambda s: (0, su_off // LANES + s)),
            per_slab(2 * S5_T - 1, LANES, LANES),
            per_slab(2, 2, S5_T, S5_GROUPS, GROUP_CH, LANES),
            per_slab(2, 2, S5_T, S5_GROUPS, GROUP_CH, LANES),
            per_slab(2, 8, SUBLANES, S5_STATES),
        ],
        out_specs=pl.BlockSpec((m, LANES), lambda s: (0, s)),
        out_shape=jax.ShapeDtypeStruct((m, slabs * LANES), F32),
        scratch_shapes=[
            pltpu.VMEM((n_chunks, row_w), BF16),
            pltpu.VMEM((row_w, row_w), BF16),
            pltpu.VMEM((row_w, 2 * S5_STATES), BF16),
            pltpu.VMEM((row_w, 2 * S5_STATES), BF16),
            pltpu.VMEM((n_chunks, 2 * S5_STATES), F32),
            pltpu.VMEM((n_chunks, row_w), F32),
        ],
        compiler_params=_cparams("parallel"),
        name="s5_mix",
    )(p, kb, pb, qc, coef)


def _s5_params(lam_re, lam_im, log_dt, b_re, b_im, c_re, c_im):
    depth, _, g, _ = lam_re.shape
    slabs = g // S5_GROUPS
    t_len = S5_T
    lr, li = lam_re.astype(F32), lam_im.astype(F32)
    dt = jnp.exp(log_dt.astype(F32))[..., None]
    ar, ai = lr * dt, li * dt

    def power(e):
        e = e.astype(F32)[:, None, :, None, None]
        mag = jnp.exp(e * ar)
        return mag * jnp.cos(e * ai), mag * jnp.sin(e * ai)

    def both(e):
        return jnp.stack([e, e], axis=1)

    l1r, l1i = power(jnp.ones((1, 2)))
    er, ei = l1r[0] - 1.0, l1i[0]
    den = lr * lr + li * li
    fr, fi = ((er * lr + ei * li) / den)[..., None], ((ei * lr - er * li) / den)[..., None]
    br, bi = b_re.astype(F32), b_im.astype(F32)
    bbr, bbi = fr * br - fi * bi, fr * bi + fi * br
    cr, ci = c_re.astype(F32), c_im.astype(F32)

    steps = jnp.arange(t_len)
    pr, pi = power(both(steps))
    wr = pr[..., None] * bbr - pi[..., None] * bbi
    wi = pr[..., None] * bbi + pi[..., None] * bbr
    kern = jnp.einsum("ldgon,tldgni->tldgio", cr, wr) - jnp.einsum("ldgon,tldgni->tldgio", ci, wi)
    kern = kern.reshape(t_len, depth, 2, slabs, LANES, GROUP_CH)
    lane_group = jnp.arange(LANES) // GROUP_CH
    same_group = (lane_group[:, None] == lane_group[None, :]).astype(F32)
    kern = jnp.tile(kern, (1, 1, 1, 1, 1, S5_GROUPS)) * same_group
    kf, kbw = kern[:, :, 0], kern[:, :, 1]
    kb = jnp.concatenate([kbw[:0:-1], (kf[0] + kbw[0])[None], kf[1:]], axis=0)
    kb = kb.transpose(1, 2, 0, 3, 4).astype(BF16)

    odd = (jnp.arange(g) % 2 == 1)[:, None, None]

    def tiles(w):
        z = jnp.zeros_like(w)
        w = jnp.where(odd, jnp.concatenate([z, w], axis=-1), jnp.concatenate([w, z], axis=-1))
        w = w.reshape(t_len, depth, 2, slabs, S5_GROUPS, GROUP_CH, LANES)
        return w.transpose(1, 3, 2, 0, 4, 5, 6)

    def parts(re, im):
        return jnp.stack([tiles(re), tiles(im)], axis=3).astype(BF16)

    pr, pi = power(jnp.stack([t_len - 1 - steps, steps], axis=1))
    pr, pi = pr[..., None, :], pi[..., None, :]
    bbr_t, bbi_t = jnp.swapaxes(bbr, -1, -2), jnp.swapaxes(bbi, -1, -2)
    pb = parts(pr * bbr_t - pi * bbi_t, pr * bbi_t + pi * bbr_t)
    pr, pi = power(jnp.stack([steps + 1, t_len - steps], axis=1))
    pr, pi = pr[..., None, :], pi[..., None, :]
    qc = parts(cr * pr - ci * pi, -(cr * pi + ci * pr))

    rows = jnp.arange(SUBLANES)

    def lay(w, keep):
        w = w * keep[:, None, :, None, None]
        return w.reshape(SUBLANES, depth, 2, slabs, S5_STATES).transpose(1, 3, 2, 0, 4)

    coefs = []
    for d in (1, 2, 4):
        keep = jnp.stack([rows >= d, rows < SUBLANES - d], axis=1).astype(F32)
        pr, pi = power(jnp.full((SUBLANES, 2), d * t_len))
        coefs += [lay(pr, keep), lay(pi, keep)]
    pr, pi = power(jnp.stack([(rows + 1) * t_len, (SUBLANES - rows) * t_len], axis=1))
    ones = jnp.ones((SUBLANES, 2), F32)
    coefs += [lay(pr, ones), lay(pi, ones)]
    coef = jnp.stack(coefs, axis=3).astype(F32)
    return kb, pb, qc, coef


def _glu_kernel(u_ref, y_ref, d_ref, w_ref, b_ref, sg_ref, o_ref):
    y = d_ref[...] * u_ref[...].astype(F32) + y_ref[...]
    z = jax.nn.gelu(y)
    g = jnp.dot(z.astype(BF16), w_ref[...], preferred_element_type=F32) + b_ref[...]
    o_ref[...] = (z * jax.nn.sigmoid(g) * _silu(sg_ref[...].astype(F32))).astype(BF16)


def _s5_glu(p, y, d_skip, w_glu, b_glu, su_off, sg_off, rows):
    w = y.shape[1]
    tm = _pick(rows, (256, 128))
    row = pl.BlockSpec((tm, w), lambda i: (i, 0))
    vec = pl.BlockSpec((1, w), lambda i: (0, 0))
    return pl.pallas_call(
        _glu_kernel,
        grid=(rows // tm,),
        in_specs=[
            pl.BlockSpec((tm, w), lambda i: (i, su_off // w)), row, vec,
            pl.BlockSpec((w, w), lambda i: (0, 0)), vec,
            pl.BlockSpec((tm, w), lambda i: (i, sg_off // w)),
        ],
        out_specs=row,
        out_shape=jax.ShapeDtypeStruct((rows, w), BF16),
        compiler_params=_cparams("parallel"),
        name="s5_glu",
    )(p, y, d_skip, w_glu, b_glu, p)


def _conv_kernel(a_ref, b_ref, ap_ref, bp_ref, an_ref, bn_ref, wdw_ref, bdw_ref, lng_ref, lnb_ref,
                 wpw_ref, bpw_ref, cg_ref, o_ref, u_ref, c_ref, *, tm, seq, n_rows):
    i = pl.program_id(0)
    row0 = i * tm
    w = u_ref.shape[1]
    has_prev = jnp.logical_and(row0 != 0, row0 != seq)
    has_next = jnp.logical_and(row0 + tm != seq, row0 + tm != n_rows)

    def glu(a, b):
        return a[...].astype(F32) * jax.nn.sigmoid(b[...].astype(F32))

    u_ref[0:CONV_HALO, :] = jnp.where(has_prev, glu(ap_ref, bp_ref), 0.0)
    u_ref[CONV_HALO:CONV_HALO + tm, :] = glu(a_ref, b_ref)
    u_ref[CONV_HALO + tm:2 * CONV_HALO + tm, :] = jnp.where(has_next, glu(an_ref, bn_ref), 0.0)

    rc = 32
    lc = _pick(w, (512, 256))
    first = CONV_HALO - CONV_K // 2

    def rows(ci, _):
        r = pl.multiple_of(ci * rc, rc)
        for l in range(0, w, lc):
            win = u_ref[pl.ds(r, rc + 2 * CONV_HALO), l:l + lc]
            acc = jnp.zeros((rc, lc), F32)
            for k in range(CONV_K):
                tap = pltpu.roll(win, rc + 2 * CONV_HALO - (first + k), 0)[0:rc, :]
                acc = acc + wdw_ref[k:k + 1, l:l + lc] * tap
            c_ref[pl.ds(r, rc), l:l + lc] = acc + bdw_ref[:, l:l + lc]
        return 0

    lax.fori_loop(0, tm // rc, rows, 0)

    c = c_ref[...]
    mu = jnp.mean(c, axis=-1, keepdims=True)
    var = jnp.mean(jnp.square(c - mu), axis=-1, keepdims=True)
    y = _silu((c - mu) * lax.rsqrt(var + NORM_EPS) * lng_ref[...] + lnb_ref[...])
    out = jnp.dot(y.astype(BF16), wpw_ref[...], preferred_element_type=F32) + bpw_ref[...]
    o_ref[...] = (out * _silu(cg_ref[...].astype(F32))).astype(BF16)


def _conv_module(p, w_dw, b_dw, ln_g, ln_b, w_pw, b_pw, seq, ca_off, cb_off, cg_off, rows):
    m = p.shape[0]
    w = w_pw.shape[0]
    tm = _pick(math.gcd(seq, m - seq), (256, 128))
    hb = tm // CONV_HALO
    n_halo = m // CONV_HALO
    ca, cb, cg = ca_off // w, cb_off // w, cg_off // w

    def prev(i):
        return jnp.maximum(i * hb - 1, 0)

    def nxt(i):
        return jnp.minimum((i + 1) * hb, n_halo - 1)

    vec = pl.BlockSpec((1, w), lambda i: (0, 0))
    return pl.pallas_call(
        functools.partial(_conv_kernel, tm=tm, seq=seq, n_rows=m),
        grid=(rows // tm,),
        in_specs=[
            pl.BlockSpec((tm, w), lambda i: (i, ca)),
            pl.BlockSpec((tm, w), lambda i: (i, cb)),
            pl.BlockSpec((CONV_HALO, w), lambda i: (prev(i), ca)),
            pl.BlockSpec((CONV_HALO, w), lambda i: (prev(i), cb)),
            pl.BlockSpec((CONV_HALO, w), lambda i: (nxt(i), ca)),
            pl.BlockSpec((CONV_HALO, w), lambda i: (nxt(i), cb)),
            pl.BlockSpec((CONV_K, w), lambda i: (0, 0)),
            vec, vec, vec,
            pl.BlockSpec((w, w), lambda i: (0, 0)),
            vec,
            pl.BlockSpec((tm, w), lambda i: (i, cg)),
        ],
        out_specs=pl.BlockSpec((tm, w), lambda i: (i, 0)),
        out_shape=jax.ShapeDtypeStruct((rows, w), BF16),
        scratch_shapes=[pltpu.VMEM((tm + 2 * CONV_HALO, w), F32), pltpu.VMEM((tm, w), F32)],
        compiler_params=_cparams("parallel"),
        name="conv_module",
    )(p, p, p, p, p, p, w_dw, b_dw, ln_g, ln_b, w_pw, b_pw, p)


def _merge_kernel(a_ref, s_ref, c_ref, wm_ref, ws_ref, wc_ref, gm_ref, gs_ref, gc_ref, o_ref):
    def branch(x_ref, w_ref, g_ref):
        y = jnp.dot(x_ref[...], w_ref[...], preferred_element_type=F32)
        return jax.nn.sigmoid(g_ref[...].astype(F32)) * y

    o_ref[...] = (branch(a_ref, wm_ref, gm_ref) + branch(s_ref, ws_ref, gs_ref)
                  + branch(c_ref, wc_ref, gc_ref)).astype(BF16)


def _merge(o_mla, o_ssm, o_conv, w_bm, w_bs, w_bc, p, gm_off, gs_off, gc_off):
    rows = o_mla.shape[0]
    d = w_bm.shape[1]
    tm = _pick(rows, (768, 640, 512, 256))
    tn = _pick(d, (512, 256))

    def lhs(x):
        return pl.BlockSpec((tm, x.shape[1]), lambda i, j: (i, 0))

    def wgt(x):
        return pl.BlockSpec((x.shape[0], tn), lambda i, j: (0, j))

    def gate(off):
        return pl.BlockSpec((tm, tn), lambda i, j: (i, off // tn + j))

    return pl.pallas_call(
        _merge_kernel,
        grid=(rows // tm, d // tn),
        in_specs=[lhs(o_mla), lhs(o_ssm), lhs(o_conv), wgt(w_bm), wgt(w_bs), wgt(w_bc),
                  gate(gm_off), gate(gs_off), gate(gc_off)],
        out_specs=pl.BlockSpec((tm, tn), lambda i, j: (i, j)),
        out_shape=jax.ShapeDtypeStruct((rows, d), BF16),
        compiler_params=_cparams("parallel", "arbitrary"),
        name="merge",
    )(o_mla, o_ssm, o_conv, w_bm, w_bs, w_bc, p, p, p)


def _outproj_kernel(m_ref, w_ref, z_ref, g_ref, gate_ref, o_ref, acc_ref, *, tm, tn, nj, seq):
    i = pl.program_id(0)
    j = pl.program_id(1)
    acc_ref[j] = jnp.dot(m_ref[...], w_ref[...], preferred_element_type=F32)

    @pl.when(j == nj - 1)
    def _():
        ss = jnp.zeros((tm, 1), F32)
        for jj in range(nj):
            t = acc_ref[jj]
            ss = ss + jnp.sum(t * t, axis=-1, keepdims=True)
        inv = lax.rsqrt(ss / (nj * tn) + NORM_EPS)
        for jj in range(nj):
            sl = slice(jj * tn, (jj + 1) * tn)
            gate = _row_select(i, tm, seq, gate_ref[0:1, sl], gate_ref[1:2, sl])
            o_ref[:, sl] = z_ref[:, sl] + gate * (acc_ref[jj] * inv * g_ref[:, sl])


def _outproj(merged, w_out, z, g_post, gate, seq):
    rows, d = merged.shape
    tm = _pick(rows, (384, 256, 128))
    tn = _pick(d, (512, 256))
    nj = d // tn
    return pl.pallas_call(
        functools.partial(_outproj_kernel, tm=tm, tn=tn, nj=nj, seq=seq),
        grid=(rows // tm, nj),
        in_specs=[
            pl.BlockSpec((tm, d), lambda i, j: (i, 0)),
            pl.BlockSpec((d, tn), lambda i, j: (0, j)),
            pl.BlockSpec((tm, d), lambda i, j: (i, 0)),
            pl.BlockSpec((1, d), lambda i, j: (0, 0)),
            pl.BlockSpec((SUBLANES, d), lambda i, j: (0, 0)),
        ],
        out_specs=pl.BlockSpec((tm, d), lambda i, j: (i, 0)),
        out_shape=jax.ShapeDtypeStruct((rows, d), F32),
        scratch_shapes=[pltpu.VMEM((nj, tm, tn), F32)],
        compiler_params=_cparams("parallel", "arbitrary"),
        name="outproj",
    )(merged, w_out, z, g_post, gate)


def _rot_cols(w):
    q = ROPE_DIM // 4
    return jnp.concatenate([-w[..., q:2 * q], w[..., 0:q], -w[..., 3 * q:4 * q], w[..., 2 * q:3 * q]], axis=-1)


def _rope_tables(seq, n_ctx):
    rows = seq // GRID_W
    row = jnp.repeat(jnp.arange(rows, dtype=F32), GRID_W)
    col = jnp.tile(jnp.arange(GRID_W, dtype=F32), rows)
    half = ROPE_DIM // 2
    inv = 1.0 / (ROPE_BASE ** (jnp.arange(0, half, 2, dtype=F32) / half))
    a0 = row[:, None] * inv
    a1 = col[:, None] * inv
    ang = jnp.concatenate([a0, a0, a1, a1], axis=1)
    cos = jnp.concatenate([jnp.cos(ang), jnp.ones((n_ctx, ROPE_DIM), F32)], axis=0)
    sin = jnp.concatenate([jnp.sin(ang), jnp.zeros((n_ctx, ROPE_DIM), F32)], axis=0)
    pad = jnp.zeros((seq + n_ctx, LANES - ROPE_DIM), F32)
    return jnp.concatenate([cos, pad], axis=1), jnp.concatenate([sin, pad], axis=1)


def kernel(x, c, ctx, c_ctx, w_mod, b_mod, g_pre, g_post, w_in, q_norm, kv_norm, w_uq, w_ukv, lam_re, lam_im, log_dt, b_re, b_im, c_re, c_im, d_skip, w_glu, b_glu, w_dw, b_dw, ln_g, ln_b, w_pw, b_pw, w_bm, w_bs, w_bc, w_out):
    batch, seq, d = x.shape
    assert batch == 1, "kernel handles the problem's batch of one"
    n_ctx = ctx.shape[1]
    m = seq + n_ctx
    depth = w_in.shape[0]
    q_rank = q_norm.shape[1]
    kv_rank = kv_norm.shape[1]
    ssm_w = d_skip.shape[1]
    conv_w = b_dw.shape[1]
    heads = w_uq.shape[2] // QK_DIM
    mla_w = heads * V_DIM
    assert ssm_w == conv_w == mla_w and ssm_w % LANES == 0 and seq % n_ctx == 0
    assert seq % (S5_T * SUBLANES) == 0 and n_ctx % (S5_T * SUBLANES) == 0

    first_col = kv_rank + ROPE_DIM
    head_w = -(-first_col // LANES) * LANES
    rest_w = mla_w + ssm_w + 3 * conv_w + 3 * d
    row_shift = first_col % LANES
    assert row_shift % 16 == 0 and row_shift > 0
    tn = _pick(math.gcd(math.gcd(first_col - row_shift, ssm_w), math.gcd(q_rank, rest_w)), (512, 256, 128))
    w_in_t = jnp.swapaxes(w_in, 1, 2)
    su_off = 0
    mg_off = ssm_w
    sg_off = mg_off + mla_w
    ca_off = sg_off + ssm_w
    cb_off = ca_off + conv_w
    cg_off = cb_off + conv_w
    gm_off = cg_off + conv_w
    gs_off = gm_off + d
    gc_off = gs_off + d
    q_off = gc_off + d

    z = jnp.concatenate([x[0], ctx[0]], axis=0)
    cvec = jnp.concatenate([c, c_ctx[None, :], jnp.zeros((SUBLANES - 1 - batch, d), F32)], axis=0)
    cos, sin = _rope_tables(seq, n_ctx)
    b_mod3 = b_mod[:, None, :]
    kb, pb, qc, coef = _s5_params(lam_re, lam_im, log_dt, b_re, b_im, c_re, c_im)
    tq = _pick(seq, (512, 256))
    tk = _pick(seq, (4096, 2048, 1024, 512, 256))

    for l in range(depth):
        last = l == depth - 1
        rows = seq if last else m
        mod = _modulation(cvec, w_mod, b_mod3, l)
        shift, scale, gate = mod[:, 0:d], mod[:, d:2 * d], mod[:, 2 * d:3 * d]
        h = _prenorm(z, g_pre[l][None, :], shift, scale, seq)
        p_head = _inproj_head(h, w_in_t, l, head_w)
        p = _inproj_body(h, w_in_t, l, first_col, ssm_w // tn, q_rank // tn, rest_w // tn, tn, row_shift)

        wq = w_uq[l].reshape(q_rank, heads, QK_DIM)
        wq = jnp.concatenate([wq, _rot_cols(wq[..., NOPE_DIM:])], axis=-1).reshape(q_rank, heads * 2 * LANES)
        wkv = w_ukv[l].reshape(kv_rank, heads, NOPE_DIM + V_DIM)
        wk = wkv[..., :NOPE_DIM].reshape(kv_rank, heads * NOPE_DIM).astype(BF16)
        wvt = wkv[..., NOPE_DIM:].transpose(1, 2, 0).reshape(heads * V_DIM, kv_rank).astype(BF16)
        q = _qup(p, q_norm[l][None, :], wq.astype(BF16), cos, sin, q_off, heads)
        k, vt = _kvup(p_head, kv_norm[l][None, :], wk, wvt, cos, sin, heads)
        o_mla = _attention(q, k, vt, p, mg_off, (0, seq), (0, m), tq, tk)
        if not last:
            o_ctx = _attention(q, k, vt, p, mg_off, (seq, n_ctx), (seq, n_ctx), n_ctx, n_ctx)
            o_mla = jnp.concatenate([o_mla, o_ctx], axis=0)

        y_ssm = _s5_mix(p, kb, pb, qc, coef, l, seq, su_off)
        o_ssm = _s5_glu(p, y_ssm, d_skip[l][None, :], w_glu[l].astype(BF16), b_glu[l][None, :],
                        su_off, sg_off, rows)

        o_conv = _conv_module(p, w_dw[l], b_dw[l][None, :], ln_g[l][None, :], ln_b[l][None, :],
                              w_pw[l].astype(BF16), b_pw[l][None, :], seq, ca_off, cb_off, cg_off, rows)

        merged = _merge(o_mla, o_ssm, o_conv, w_bm[l].astype(BF16), w_bs[l].astype(BF16), w_bc[l].astype(BF16),
                        p, gm_off, gs_off, gc_off)
        z = _outproj(merged, w_out[l].astype(BF16), z, g_post[l][None, :], gate, seq)

    return z[None]
```
